```python
import math
import jax, jax.numpy as jnp
from jax import lax
import numpy as np

D_MODEL = 1024
BATCH = 32
SEQ = 2048
DEPTH = 2

N_MEM = 256
EPS = 1e-6

POOL_WINDOWS = (2, 4, 8, 16)
POOL_MAX_W = max(POOL_WINDOWS)
POOL_GROUPS = len(POOL_WINDOWS)
POOL_WIDTH = D_MODEL // 2
POOL_GROUP_DIM = POOL_WIDTH // POOL_GROUPS
SGU_WIDTH = D_MODEL // 2
SGU_GROUPS = 4
SGU_GROUP_DIM = SGU_WIDTH // SGU_GROUPS
SGU_CHUNK = 128
EVEN_IN = POOL_WIDTH + 2 * SGU_WIDTH

DIL_HEADS = 8
DIL_HEAD_DIM = 64
DIL_WIDTH = DIL_HEADS * DIL_HEAD_DIM
DIL_PAIRS = ((128, 1), (512, 4), (2048, 16))
DIL_BLOCK = 128
ROPE_THETA = 500000.0
ROPE_DIM = DIL_HEAD_DIM // 4
RET_HEADS = 4
RET_HEAD_DIM = 128
RET_WIDTH = RET_HEADS * RET_HEAD_DIM
RET_CHUNK = 128
RET_THETA = 10000.0
ODD_SIZES = (DIL_WIDTH, DIL_WIDTH, DIL_WIDTH, RET_WIDTH, RET_WIDTH, RET_WIDTH, RET_WIDTH)
ODD_IN = sum(ODD_SIZES)

XATTN_HEADS = 4
XATTN_HEAD_DIM = D_MODEL // XATTN_HEADS
D_FF = -(-8 * D_MODEL // (3 * 256)) * 256

N_EVEN = (DEPTH + 1) // 2
N_ODD = DEPTH // 2

kernel_name = "hybrid_pool_sgu_dilated_retention_block"


def rmsnorm(x, g):
    xf = x.astype(jnp.float32)
    y = xf * lax.rsqrt(jnp.mean(xf * xf, axis=-1, keepdims=True) + EPS)
    return (y * g.astype(jnp.float32)).astype(x.dtype)


def rotary(x, pos, rot_dim, theta):
    half = rot_dim // 2
    inv = 1.0 / jnp.power(jnp.float32(theta), jnp.arange(half, dtype=jnp.float32) / half)
    ang = pos.astype(jnp.float32)[:, None] * inv[None, :]
    cos = jnp.cos(ang)[None, :, None, :]
    sin = jnp.sin(ang)[None, :, None, :]
    xr = x[..., :rot_dim].astype(jnp.float32)
    x1, x2 = xr[..., :half], xr[..., half:]
    rot = jnp.concatenate([x1 * cos - x2 * sin, x1 * sin + x2 * cos], axis=-1).astype(x.dtype)
    return jnp.concatenate([rot, x[..., rot_dim:]], axis=-1)


def pool_mixer(a, w, scale):
    B, S, _ = a.shape
    ag = a.reshape(B, S, POOL_GROUPS, POOL_GROUP_DIM).astype(jnp.float32)
    cs = jnp.pad(jnp.cumsum(ag, axis=1), ((0, 0), (POOL_MAX_W, 0), (0, 0), (0, 0)))
    t = jnp.arange(S)
    outs = []
    for g, win in enumerate(POOL_WINDOWS):
        csg = cs[:, :, g]
        wsum = csg[:, POOL_MAX_W:] - csg[:, POOL_MAX_W - win:POOL_MAX_W - win + S]
        cnt = jnp.minimum(t + 1, win).astype(jnp.float32)[None, :, None]
        outs.append(wsum / cnt - ag[:, :, g])
    d = jnp.stack(outs, axis=2).astype(a.dtype)
    y = jnp.einsum('bsgc,gcd->bsgd', d, w)
    return y.reshape(B, S, POOL_WIDTH) * scale


def sgu_mixer(u, v, norm_g, w_s, b_s):
    B, S, _ = u.shape
    n = S // SGU_CHUNK
    vc = rmsnorm(v, norm_g).reshape(B, n, SGU_CHUNK, SGU_GROUPS, SGU_GROUP_DIM)
    causal = jnp.tril(jnp.ones((SGU_CHUNK, SGU_CHUNK), dtype=bool))
    ws = jnp.where(causal[None], w_s, jnp.zeros_like(w_s))
    mixed = jnp.einsum('gts,bnsgc->bntgc', ws, vc) + jnp.transpose(b_s)[None, None, :, :, None]
    return u * mixed.reshape(B, S, SGU_WIDTH)


def even_mixer(h, w_in, pool_w, pool_scale, sgu_norm, sgu_w, sgu_b, w_out):
    z = h @ w_in
    a = z[..., :POOL_WIDTH]
    uv = jax.nn.gelu(z[..., POOL_WIDTH:])
    u, v = uv[..., :SGU_WIDTH], uv[..., SGU_WIDTH:]
    ya = pool_mixer(a, pool_w, pool_scale)
    yb = sgu_mixer(u, v, sgu_norm, sgu_w, sgu_b)
    return jnp.concatenate([ya, yb], axis=-1) @ w_out


def dilated_branch(q, k, v, dil, steps):
    B, S, H, dh = q.shape
    L = -(-S // dil)
    L = -(-L // DIL_BLOCK) * DIL_BLOCK
    nb = L // DIL_BLOCK
    S_pad = L * dil
    pad = ((0, 0), (0, S_pad - S), (0, 0), (0, 0))

    def sub(t):
        t = jnp.pad(t, pad).reshape(B, L, dil, H, dh)
        return jnp.moveaxis(t, 2, 1).reshape(B, dil, nb, DIL_BLOCK, H, dh)

    def with_prev(t):
        prev = jnp.pad(t, ((0, 0), (0, 0), (1, 0), (0, 0), (0, 0), (0, 0)))[:, :, :-1]
        return jnp.concatenate([prev, t], axis=3)

    qs = sub(q)
    kk = with_prev(sub(k))
    vv = with_prev(sub(v))
    scores = jnp.einsum('brnqhd,brnkhd->brnhqk', qs, kk).astype(jnp.float32) * (dh ** -0.5)
    qi = jnp.arange(DIL_BLOCK)[:, None] + DIL_BLOCK
    ki = jnp.arange(2 * DIL_BLOCK)[None, :]
    dist = qi - ki
    band = (dist >= 0) & (dist <= steps)
    has_prev = (jnp.arange(nb) > 0)[:, None, None]
    mask = band[None] & (has_prev | (ki >= DIL_BLOCK)[None])
    scores = jnp.where(mask[None, None, :, None], scores, -jnp.inf)
    m = jnp.max(scores, axis=-1, keepdims=True)
    p = jnp.exp(scores - m)
    s = jnp.sum(p, axis=-1)
    o = jnp.einsum('brnhqk,brnkhd->brnqhd', p, vv.astype(jnp.float32))
    o = o / jnp.swapaxes(s, 3, 4)[..., None]
    lse = jnp.swapaxes(m[..., 0] + jnp.log(s), 3, 4)

    def unsub(t):
        t = t.reshape((B, dil, L) + t.shape[4:])
        t = jnp.moveaxis(t, 1, 2)
        return t.reshape((B, S_pad) + t.shape[3:])[:, :S]

    return unsub(o), unsub(lse)


def dilated_mixture(q, k, v, pos):
    q = rotary(q, pos, ROPE_DIM, ROPE_THETA)
    k = rotary(k, pos, ROPE_DIM, ROPE_THETA)
    outs, lses = [], []
    for window, dil in DIL_PAIRS:
        o, lse = dilated_branch(q, k, v, dil, window // dil)
        outs.append(o)
        lses.append(lse)
    wts = jax.nn.softmax(jnp.stack(lses, axis=0), axis=0)
    return jnp.einsum('gbsh,gbshd->bshd', wts, jnp.stack(outs, axis=0))


def retention(q, k, v, pos):
    B, S, H, dk = q.shape
    dv = v.shape[-1]
    C = RET_CHUNK
    n = S // C
    q = rotary(q, pos, dk, RET_THETA).astype(jnp.float32)
    k = rotary(k, pos, dk, RET_THETA).astype(jnp.float32) * (dk ** -0.5)
    v = v.astype(jnp.float32)
    gamma = 1.0 - jnp.power(2.0, -5.0 - jnp.arange(H, dtype=jnp.float32))
    log_g = jnp.log(gamma)
    qc = q.reshape(B, n, C, H, dk)
    kc = k.reshape(B, n, C, H, dk)
    vc = v.reshape(B, n, C, H, dv)
    idx = jnp.arange(C, dtype=jnp.float32)
    rel = idx[:, None] - idx[None, :]
    decay = jnp.where(rel[None] >= 0, jnp.exp(jnp.maximum(rel, 0.0)[None] * log_g[:, None, None]), 0.0)
    inner = jnp.einsum('bnihd,bnjhd->bnhij', qc, kc) * decay[None, None]
    y_inner = jnp.einsum('bnhij,bnjhe->bnihe', inner, vc)
    k_decay = jnp.exp((C - 1 - idx)[None, :] * log_g[:, None])
    q_decay = jnp.exp((idx + 1)[None, :] * log_g[:, None])
    chunk_kv = jnp.einsum('bnjhd,hj,bnjhe->bnhde', kc, k_decay, vc)
    g_chunk = jnp.exp(C * log_g)[None, :, None, None]

    def step(state, kv):
        return state * g_chunk + kv, state

    init = jnp.zeros((B, H, dk, dv), jnp.float32)
    _, prev = lax.scan(step, init, jnp.moveaxis(chunk_kv, 1, 0))
    prev = jnp.moveaxis(prev, 0, 1)
    y_cross = jnp.einsum('bnihd,bnhde,hi->bnihe', qc, prev, q_decay)
    y = (y_inner + y_cross).reshape(B, S, H, dv)
    mu = jnp.mean(y, axis=-1, keepdims=True)
    var = jnp.mean(jnp.square(y - mu), axis=-1, keepdims=True)
    return (y - mu) * lax.rsqrt(var + EPS)


def odd_mixer(h, pos, w_in, w_out):
    B, S, _ = h.shape
    z = h @ w_in
    cuts = [int(c) for c in np.cumsum(ODD_SIZES)[:-1]]
    cq, ck, cv, rq, rk, rv, rg = jnp.split(z, cuts, axis=-1)
    dshape = (B, S, DIL_HEADS, DIL_HEAD_DIM)
    rshape = (B, S, RET_HEADS, RET_HEAD_DIM)
    yc = dilated_mixture(cq.reshape(dshape), ck.reshape(dshape), cv.reshape(dshape), pos)
    yc = yc.reshape(B, S, DIL_WIDTH).astype(h.dtype)
    yr = retention(rq.reshape(rshape), rk.reshape(rshape), rv.reshape(rshape), pos)
    yd = jax.nn.silu(rg) * yr.reshape(B, S, RET_WIDTH).astype(h.dtype)
    return jnp.concatenate([yc, yd], axis=-1) @ w_out


def memory_cross_attention(h, mem_n, wq, wkv, wo):
    B, S, _ = h.shape
    M = mem_n.shape[1]
    q = (h @ wq).reshape(B, S, XATTN_HEADS, XATTN_HEAD_DIM)
    kv = (mem_n @ wkv).reshape(B, M, 2, XATTN_HEADS, XATTN_HEAD_DIM)
    k, v = kv[:, :, 0], kv[:, :, 1]
    s = jnp.einsum('bqhd,bkhd->bhqk', q, k).astype(jnp.float32) * (XATTN_HEAD_DIM ** -0.5)
    p = jax.nn.softmax(s, axis=-1).astype(v.dtype)
    o = jnp.einsum('bhqk,bkhd->bqhd', p, v).reshape(B, S, D_MODEL)
    return o @ wo


def swiglu(h, w_gate_up, w_down):
    gu = h @ w_gate_up
    g, u = gu[..., :D_FF], gu[..., D_FF:]
    return (jax.nn.silu(g) * u) @ w_down


def setup_inputs(seed: int = 0) -> dict:
    key = jax.random.key(seed)
    ks = iter(jax.random.split(key, 32))
    f32 = jnp.float32

    def normal(shape, fan_in):
        return jax.random.normal(next(ks), shape, f32) * (fan_in ** -0.5)

    def gain(shape):
        return 1.0 + 0.02 * jax.random.normal(next(ks), shape, f32)

    return {
        "x": jax.random.normal(next(ks), (BATCH, SEQ, D_MODEL), f32),
        "mem": jax.random.normal(next(ks), (BATCH, N_MEM, D_MODEL), f32),
        "even_mix_norm": gain((N_EVEN, D_MODEL)),
        "even_w_in": normal((N_EVEN, D_MODEL, EVEN_IN), D_MODEL),
        "pool_w": normal((N_EVEN, POOL_GROUPS, POOL_GROUP_DIM, POOL_GROUP_DIM), POOL_GROUP_DIM),
        "pool_scale": gain((N_EVEN, POOL_WIDTH)),
        "sgu_norm": gain((N_EVEN, SGU_WIDTH)),
        "sgu_w": normal((N_EVEN, SGU_GROUPS, SGU_CHUNK, SGU_CHUNK), SGU_CHUNK),
        "sgu_b": gain((N_EVEN, SGU_GROUPS, SGU_CHUNK)),
        "even_w_out": normal((N_EVEN, POOL_WIDTH + SGU_WIDTH, D_MODEL), POOL_WIDTH + SGU_WIDTH),
        "odd_mix_norm": gain((N_ODD, D_MODEL)),
        "odd_w_in": normal((N_ODD, D_MODEL, ODD_IN), D_MODEL),
        "odd_w_out": normal((N_ODD, DIL_WIDTH + RET_WIDTH, D_MODEL), DIL_WIDTH + RET_WIDTH),
        "xattn_norm": gain((DEPTH, D_MODEL)),
        "mem_norm": gain((DEPTH, D_MODEL)),
        "xattn_wq": normal((DEPTH, D_MODEL, D_MODEL), D_MODEL),
        "xattn_wkv": normal((DEPTH, D_MODEL, 2 * D_MODEL), D_MODEL),
        "xattn_wo": normal((DEPTH, D_MODEL, D_MODEL), D_MODEL),
        "ffn_norm": gain((DEPTH, D_MODEL)),
        "ffn_w_gate_up": normal((DEPTH, D_MODEL, 2 * D_FF), D_MODEL),
        "ffn_w_down": normal((DEPTH, D_FF, D_MODEL), D_FF),
        "final_norm": gain((D_MODEL,)),
    }


def reference(x, mem, even_mix_norm, even_w_in, pool_w, pool_scale, sgu_norm, sgu_w, sgu_b,
              even_w_out, odd_mix_norm, odd_w_in, odd_w_out, xattn_norm, mem_norm, xattn_wq,
              xattn_wkv, xattn_wo, ffn_norm, ffn_w_gate_up, ffn_w_down, final_norm):
    pos = jnp.arange(x.shape[1], dtype=jnp.int32)
    h = x
    for layer in range(DEPTH):
        i = layer // 2
        if layer % 2 == 0:
            h = h + even_mixer(rmsnorm(h, even_mix_norm[i]), even_w_in[i], pool_w[i], pool_scale[i],
                               sgu_norm[i], sgu_w[i], sgu_b[i], even_w_out[i])
        else:
            h = h + odd_mixer(rmsnorm(h, odd_mix_norm[i]), pos, odd_w_in[i], odd_w_out[i])
        h = h + memory_cross_attention(rmsnorm(h, xattn_norm[layer]), rmsnorm(mem, mem_norm[layer]),
                                       xattn_wq[layer], xattn_wkv[layer], xattn_wo[layer])
        h = h + swiglu(rmsnorm(h, ffn_norm[layer]), ffn_w_gate_up[layer], ffn_w_down[layer])
    return rmsnorm(h, final_norm)
```

```python
import functools
import math

import jax
import jax.numpy as jnp
from jax import lax
from jax.experimental import pallas as pl
from jax.experimental.pallas import tpu as pltpu

F32 = jnp.float32
BF16 = jnp.bfloat16

EPS = 1e-6
D_MODEL = 1024
N_MEM = 256

POOL_WINDOWS = (2, 4, 8, 16)
POOL_HALO = 16
POOL_WIDTH = 512
GROUP_DIM = 128
SGU_WIDTH = 512
SGU_CHUNK = 128
EVEN_IN = POOL_WIDTH + 2 * SGU_WIDTH

DIL_HEADS = 8
DIL_HEAD_DIM = 64
DIL_WIDTH = 512
DIL_PAIRS = ((128, 1), (512, 4), (2048, 16))
DIL_BLOCK = 128
ROPE_THETA = 500000.0
ROPE_DIM = 16
RET_HEADS = 4
RET_HEAD_DIM = 128
RET_WIDTH = 512
RET_CHUNK = 128
RET_THETA = 10000.0
ODD_IN = 3 * DIL_WIDTH + 4 * RET_WIDTH

XATTN_HEADS = 4
XATTN_HEAD_DIM = 256
D_FF = 2816
FF_CHUNK = 1408

LANES = 128
SEQ_TILE = 512
VMEM_LIMIT = 56 * 1024 * 1024


def _params(*sem):
    return pltpu.CompilerParams(dimension_semantics=sem, vmem_limit_bytes=VMEM_LIMIT)


def _const_spec(shape):
    zeros = (0,) * len(shape)
    return pl.BlockSpec(shape, lambda *_: zeros, pipeline_mode=pl.Buffered(1))


def _rms(x, g):
    ms = jnp.mean(x * x, axis=-1, keepdims=True)
    return x * lax.rsqrt(ms + EPS) * g


def _gelu_tanh(x):
    return 0.5 * x * (1.0 + jnp.tanh(math.sqrt(2.0 / math.pi) * (x + 0.044715 * (x * x * x))))


def _silu(x):
    return x * (1.0 / (1.0 + jnp.exp(-x)))


def _dot(a, b):
    return jnp.dot(a, b, preferred_element_type=F32)


def _dot_nt(a, b):
    return lax.dot_general(a, b, (((1,), (1,)), ((), ())), preferred_element_type=F32)


def _dot_tn(a, b):
    return lax.dot_general(a, b, (((0,), (0,)), ((), ())), preferred_element_type=F32)


def _even_kernel(x_ref, g_ref, win_ref, pw_ref, ps_ref, sn_ref, sw_ref, sb_ref, wout_ref,
                 o_ref, ext_ref, ycat_ref):
    si = pl.program_id(1)
    ts = x_ref.shape[1]
    x = x_ref[0]
    hn = _rms(x, g_ref[...]).astype(BF16)

    a = _dot(hn, win_ref[:, :POOL_WIDTH])

    @pl.when(si == 0)
    def _():
        ext_ref[0:POOL_HALO, :] = jnp.zeros((POOL_HALO, POOL_WIDTH), F32)

    ext_ref[POOL_HALO:, :] = a
    pos = si * ts + lax.broadcasted_iota(jnp.int32, (ts, 1), 0)
    for g, win in enumerate(POOL_WINDOWS):
        cols = slice(g * GROUP_DIM, (g + 1) * GROUP_DIM)
        ag = a[:, cols]
        wsum = ag
        for j in range(1, win):
            wsum = wsum + ext_ref[POOL_HALO - j:POOL_HALO - j + ts, cols]
        cnt = jnp.minimum(pos + 1, win).astype(F32)
        d = (wsum / cnt - ag).astype(BF16)
        ya = _dot(d, pw_ref[g]) * ps_ref[:, cols]
        ycat_ref[:, cols] = ya.astype(BF16)
    ext_ref[0:POOL_HALO, :] = ext_ref[ts:ts + POOL_HALO, :]

    u = _gelu_tanh(_dot(hn, win_ref[:, POOL_WIDTH:POOL_WIDTH + SGU_WIDTH]))
    v = _gelu_tanh(_dot(hn, win_ref[:, POOL_WIDTH + SGU_WIDTH:]))
    vn = _rms(v, sn_ref[...]).astype(BF16)
    nc = ts // SGU_CHUNK
    row = lax.broadcasted_iota(jnp.int32, (SGU_CHUNK, SGU_CHUNK), 0)
    col = lax.broadcasted_iota(jnp.int32, (SGU_CHUNK, SGU_CHUNK), 1)
    for g in range(SGU_WIDTH // GROUP_DIM):
        cols = slice(g * GROUP_DIM, (g + 1) * GROUP_DIM)
        ws = jnp.where(row >= col, sw_ref[g], 0.0).astype(BF16)
        rhs = jnp.concatenate([vn[c * SGU_CHUNK:(c + 1) * SGU_CHUNK, cols] for c in range(nc)], axis=1)
        mixed = _dot(ws, rhs) + sb_ref[:, g:g + 1]
        for c in range(nc):
            rows = slice(c * SGU_CHUNK, (c + 1) * SGU_CHUNK)
            yb = u[rows, cols] * mixed[:, c * SGU_CHUNK:(c + 1) * SGU_CHUNK]
            ycat_ref[rows, POOL_WIDTH + g * GROUP_DIM:POOL_WIDTH + (g + 1) * GROUP_DIM] = yb.astype(BF16)

    o_ref[0] = x + _dot(ycat_ref[...], wout_ref[...])


def _even_mixer(h, g, w_in, pool_w, pool_scale, sgu_norm, sgu_w, sgu_b_t, w_out):
    b, s, d = h.shape
    ts = SEQ_TILE
    return pl.pallas_call(
        _even_kernel,
        grid=(b, s // ts),
        in_specs=[
            pl.BlockSpec((1, ts, d), lambda i, j: (i, j, 0)),
            _const_spec((1, d)),
            _const_spec((d, EVEN_IN)),
            _const_spec(pool_w.shape),
            _const_spec((1, POOL_WIDTH)),
            _const_spec((1, SGU_WIDTH)),
            _const_spec(sgu_w.shape),
            _const_spec(sgu_b_t.shape),
            _const_spec((POOL_WIDTH + SGU_WIDTH, d)),
        ],
        out_specs=pl.BlockSpec((1, ts, d), lambda i, j: (i, j, 0)),
        out_shape=jax.ShapeDtypeStruct(h.shape, F32),
        scratch_shapes=[
            pltpu.VMEM((POOL_HALO + ts, POOL_WIDTH), F32),
            pltpu.VMEM((ts, POOL_WIDTH + SGU_WIDTH), BF16),
        ],
        compiler_params=_params("arbitrary", "arbitrary"),
        name="even_mixer",
    )(h, g, w_in, pool_w, pool_scale, sgu_norm, sgu_w, sgu_b_t, w_out)


def _kv_kernel(m_ref, g_ref, w_ref, o_ref):
    mn = _rms(m_ref[0], g_ref[0]).astype(BF16)
    o_ref[0, 0] = _dot(mn, w_ref[0]).astype(BF16)


def _memory_kv(mem, mem_norm, wkv):
    b, m, d = mem.shape
    depth = wkv.shape[0]
    return pl.pallas_call(
        _kv_kernel,
        grid=(depth, b),
        in_specs=[
            pl.BlockSpec((1, m, d), lambda l, i: (i, 0, 0)),
            pl.BlockSpec((1, 1, d), lambda l, i: (l, 0, 0)),
            pl.BlockSpec((1, d, 2 * d), lambda l, i: (l, 0, 0)),
        ],
        out_specs=pl.BlockSpec((1, 1, m, 2 * d), lambda l, i: (l, i, 0, 0)),
        out_shape=jax.ShapeDtypeStruct((depth, b, m, 2 * d), BF16),
        compiler_params=_params("arbitrary", "arbitrary"),
        name="memory_kv",
    )(mem, mem_norm.reshape(depth, 1, d), wkv)


def _xattn_body(x, kv_ref, g_ref, wq_ref, wo_ref, o_ref, ocat_ref):
    hn = _rms(x, g_ref[...]).astype(BF16)
    q = (_dot(hn, wq_ref[...]) * (XATTN_HEAD_DIM ** -0.5)).astype(BF16)
    for h in range(XATTN_HEADS):
        cols = slice(h * XATTN_HEAD_DIM, (h + 1) * XATTN_HEAD_DIM)
        kh = kv_ref[0, 0, :, cols]
        vh = kv_ref[0, 0, :, D_MODEL + h * XATTN_HEAD_DIM:D_MODEL + (h + 1) * XATTN_HEAD_DIM]
        s = _dot_nt(q[:, cols], kh)
        p = jnp.exp(s - jnp.max(s, axis=-1, keepdims=True))
        l = jnp.sum(p, axis=-1, keepdims=True)
        ocat_ref[:, cols] = (_dot(p.astype(BF16), vh) * (1.0 / l)).astype(BF16)
    o_ref[0] = x + _dot(ocat_ref[...], wo_ref[...])


def _xattn_kernel(h_ref, kv_ref, g_ref, wq_ref, wo_ref, o_ref, ocat_ref):
    _xattn_body(h_ref[0], kv_ref, g_ref, wq_ref, wo_ref, o_ref, ocat_ref)


def _mix_xattn_kernel(h_ref, yc_ref, yd_ref, wmix_ref, kv_ref, g_ref, wq_ref, wo_ref, o_ref, ocat_ref):
    x = h_ref[0] + _dot(yc_ref[0], wmix_ref[:DIL_WIDTH, :]) + _dot(yd_ref[0], wmix_ref[DIL_WIDTH:, :])
    _xattn_body(x, kv_ref, g_ref, wq_ref, wo_ref, o_ref, ocat_ref)


def _xattn(h, kv, layer, g, wq, wo, mix=None):
    b, s, d = h.shape
    ts = SEQ_TILE
    tile = pl.BlockSpec((1, ts, d), lambda i, j: (i, j, 0))
    kv_spec = pl.BlockSpec((1, 1, N_MEM, 2 * d), lambda i, j: (layer, i, 0, 0))
    tail_specs = [kv_spec, _const_spec((1, d)), _const_spec((d, d)), _const_spec((d, d))]
    if mix is None:
        kern, in_specs, args = _xattn_kernel, [tile] + tail_specs, (h, kv, g, wq, wo)
    else:
        yc, yd, w_mix = mix
        half = pl.BlockSpec((1, ts, DIL_WIDTH), lambda i, j: (i, j, 0))
        kern = _mix_xattn_kernel
        in_specs = [tile, half, half, _const_spec((d, d))] + tail_specs
        args = (h, yc, yd, w_mix, kv, g, wq, wo)
    return pl.pallas_call(
        kern,
        grid=(b, s // ts),
        in_specs=in_specs,
        out_specs=tile,
        out_shape=jax.ShapeDtypeStruct(h.shape, F32),
        scratch_shapes=[pltpu.VMEM((ts, d), BF16)],
        compiler_params=_params("arbitrary", "arbitrary"),
        name="xattn" if mix is None else "mix_xattn",
    )(*args)


def _ffn_kernel(x_ref, g_ref, wgu_ref, wd_ref, fg_ref, o_ref, *, final_norm):
    x = x_ref[...]
    hn = _rms(x, g_ref[...]).astype(BF16)
    acc = x
    for c in range(D_FF // FF_CHUNK):
        gate = _dot(hn, wgu_ref[:, c * FF_CHUNK:(c + 1) * FF_CHUNK])
        up = _dot(hn, wgu_ref[:, D_FF + c * FF_CHUNK:D_FF + (c + 1) * FF_CHUNK])
        act = (_silu(gate) * up).astype(BF16)
        acc = acc + _dot(act, wd_ref[c * FF_CHUNK:(c + 1) * FF_CHUNK, :])
    if final_norm:
        acc = _rms(acc, fg_ref[...])
    o_ref[...] = acc


def _ffn(h, g, wgu, wd, fg, final_norm):
    b, s, d = h.shape
    t = b * s
    tm = SEQ_TILE
    tile = pl.BlockSpec((tm, d), lambda i: (i, 0))
    out = pl.pallas_call(
        functools.partial(_ffn_kernel, final_norm=final_norm),
        grid=(t // tm,),
        in_specs=[tile, _const_spec((1, d)), _const_spec((d, 2 * D_FF)), _const_spec((D_FF, d)),
                  _const_spec((1, d))],
        out_specs=tile,
        out_shape=jax.ShapeDtypeStruct((t, d), F32),
        compiler_params=_params("arbitrary"),
        name="ffn_final" if final_norm else "ffn",
    )(h.reshape(t, d), g, wgu, wd, fg)
    return out.reshape(b, s, d)


def _odd_proj_kernel(x_ref, g_ref, w_ref, dc_ref, dsa_ref, dsb_ref, rc_ref, rs_ref,
                     cq_ref, ck_ref, cv_ref, rq_ref, rk_ref, rv_ref, rg_ref):
    hn = _rms(x_ref[0], g_ref[...]).astype(BF16)

    def section(i):
        return _dot(hn, w_ref[:, i * DIL_WIDTH:(i + 1) * DIL_WIDTH])

    def lane_blocks(z, fn):
        return jnp.concatenate([fn(z[:, j * LANES:(j + 1) * LANES]) for j in range(z.shape[1] // LANES)], axis=1)

    half = ROPE_DIM // 2
    dc, dsa, dsb = dc_ref[...], dsa_ref[...], dsb_ref[...]

    def dil_rot(zb):
        return zb * dc + pltpu.roll(zb, half, 1) * dsa + pltpu.roll(zb, LANES - half, 1) * dsb

    cq_ref[0] = lane_blocks(section(0), dil_rot)
    ck_ref[0] = lane_blocks(section(1), dil_rot)
    cv_ref[0] = section(2)

    rc, rs = rc_ref[...], rs_ref[...]

    def ret_rot(zb):
        return zb * rc + pltpu.roll(zb, RET_HEAD_DIM // 2, 1) * rs

    rq_ref[0] = lane_blocks(section(3), ret_rot).astype(BF16)
    rk_ref[0] = (lane_blocks(section(4), ret_rot) * (RET_HEAD_DIM ** -0.5)).astype(BF16)
    rv_ref[0] = section(5).astype(BF16)
    rg_ref[0] = section(6)


def _rotary_tables(s):
    pos = jnp.arange(s, dtype=jnp.int32).astype(F32)
    lane = jnp.arange(LANES)
    half = ROPE_DIM // 2
    inv = 1.0 / jnp.power(jnp.float32(ROPE_THETA), jnp.arange(half, dtype=F32) / half)
    ang = pos[:, None] * inv[None, :]
    cos, sin = jnp.cos(ang), jnp.sin(ang)
    hl = lane % DIL_HEAD_DIM
    cos_l, sin_l = cos[:, hl % half], sin[:, hl % half]
    dc = jnp.where(hl[None] < ROPE_DIM, cos_l, 1.0)
    dsa = jnp.where((hl[None] >= half) & (hl[None] < ROPE_DIM), sin_l, 0.0)
    dsb = jnp.where(hl[None] < half, -sin_l, 0.0)
    rhalf = RET_HEAD_DIM // 2
    rinv = 1.0 / jnp.power(jnp.float32(RET_THETA), jnp.arange(rhalf, dtype=F32) / rhalf)
    rang = pos[:, None] * rinv[None, :]
    rcos, rsin = jnp.cos(rang), jnp.sin(rang)
    rc = jnp.concatenate([rcos, rcos], axis=1)
    rs = jnp.concatenate([-rsin, rsin], axis=1)
    return dc, dsa, dsb, rc, rs


def _odd_proj(h, g, w_in):
    b, s, d = h.shape
    ts = SEQ_TILE
    tables = _rotary_tables(s)
    tile = pl.BlockSpec((1, ts, d), lambda j, i: (i, j, 0))
    tab = pl.BlockSpec((ts, LANES), lambda j, i: (j, 0))
    out = pl.BlockSpec((1, ts, DIL_WIDTH), lambda j, i: (i, j, 0))
    f32o = jax.ShapeDtypeStruct((b, s, DIL_WIDTH), F32)
    bf16o = jax.ShapeDtypeStruct((b, s, RET_WIDTH), BF16)
    return pl.pallas_call(
        _odd_proj_kernel,
        grid=(s // ts, b),
        in_specs=[tile, _const_spec((1, d)), _const_spec((d, ODD_IN))] + [tab] * 5,
        out_specs=[out] * 7,
        out_shape=[f32o, f32o, f32o, bf16o, bf16o, bf16o, f32o],
        compiler_params=_params("arbitrary", "arbitrary"),
        name="odd_proj",
    )(h, g, w_in, *tables)


def _dilated_kernel(q_ref, k_ref, v_ref, o_ref, acc_ref, m_ref, l_ref):
    s_len = q_ref.shape[1]
    blk = DIL_BLOCK
    lane = lax.broadcasted_iota(jnp.int32, (blk, LANES), 1)
    first_head = lane < DIL_HEAD_DIM
    qi = lax.broadcasted_iota(jnp.int32, (2 * blk, blk), 0) % blk
    ki = lax.broadcasted_iota(jnp.int32, (2 * blk, blk), 1)
    cur_mask = ki <= qi
    prev_mask = ki >= qi

    def rows(start, dil):
        return pl.ds(start, blk) if dil == 1 else pl.ds(start, blk, stride=dil)

    def block(branch, dil, start, with_prev):
        r_cur = rows(start, dil)
        q = q_ref[0, r_cur, :] * (DIL_HEAD_DIM ** -0.5)
        q2 = jnp.concatenate([jnp.where(first_head, q, 0.0), jnp.where(first_head, 0.0, q)], axis=0).astype(BF16)
        kc = k_ref[0, r_cur, :].astype(BF16)
        vc = v_ref[0, r_cur, :].astype(BF16)
        sc = jnp.where(cur_mask, _dot_nt(q2, kc), -jnp.inf)
        if with_prev:
            r_prev = rows(start - dil * blk, dil)
            kp = k_ref[0, r_prev, :].astype(BF16)
            vp = v_ref[0, r_prev, :].astype(BF16)
            sp = jnp.where(prev_mask, _dot_nt(q2, kp), -jnp.inf)
            m = jnp.maximum(jnp.max(sc, axis=-1, keepdims=True), jnp.max(sp, axis=-1, keepdims=True))
            pc = jnp.exp(sc - m)
            pp = jnp.exp(sp - m)
            l = jnp.sum(pc, axis=-1, keepdims=True) + jnp.sum(pp, axis=-1, keepdims=True)
            o2 = _dot(pc.astype(BF16), vc) + _dot(pp.astype(BF16), vp)
        else:
            m = jnp.max(sc, axis=-1, keepdims=True)
            pc = jnp.exp(sc - m)
            l = jnp.sum(pc, axis=-1, keepdims=True)
            o2 = _dot(pc.astype(BF16), vc)
        acc_ref[branch, r_cur, :] = jnp.where(first_head, o2[:blk], o2[blk:])
        m_ref[branch, r_cur, :] = jnp.where(first_head, m[:blk], m[blk:])
        l_ref[branch, r_cur, :] = jnp.where(first_head, l[:blk], l[blk:])

    for branch, (window, dil) in enumerate(DIL_PAIRS):
        assert window // dil == blk
        nb = s_len // (dil * blk)

        def residue(r, carry, branch=branch, dil=dil, nb=nb):
            block(branch, dil, r, False)

            def step(n, c):
                block(branch, dil, r + pl.multiple_of(n * (dil * blk), blk), True)
                return c

            return lax.fori_loop(1, nb, step, carry)

        if dil == 1:
            residue(0, 0)
        else:
            lax.fori_loop(0, dil, residue, 0)

    m_all = jnp.maximum(jnp.maximum(m_ref[0], m_ref[1]), m_ref[2])
    num = jnp.zeros((s_len, LANES), F32)
    den = jnp.zeros((s_len, LANES), F32)
    for branch in range(len(DIL_PAIRS)):
        w = jnp.exp(m_ref[branch] - m_all)
        num = num + w * acc_ref[branch]
        den = den + w * l_ref[branch]
    o_ref[0] = (num / den).astype(BF16)


def _dilated_attention(cq, ck, cv):
    b, s, _ = cq.shape
    spec = pl.BlockSpec((1, s, LANES), lambda i, p: (i, 0, p))
    nbr = len(DIL_PAIRS)
    return pl.pallas_call(
        _dilated_kernel,
        grid=(b, DIL_WIDTH // LANES),
        in_specs=[spec] * 3,
        out_specs=spec,
        out_shape=jax.ShapeDtypeStruct((b, s, DIL_WIDTH), BF16),
        scratch_shapes=[pltpu.VMEM((nbr, s, LANES), F32)] * 3,
        compiler_params=_params("arbitrary", "arbitrary"),
        name="dilated_attention",
    )(cq, ck, cv)


def _retention_kernel(q_ref, k_ref, v_ref, g_ref, dec_ref, kdec_ref, qdec_ref, o_ref):
    s_len = q_ref.shape[1]
    c_len = RET_CHUNK
    decay = dec_ref[0]
    kdec = kdec_ref[0]
    qdec = qdec_ref[0]
    g_chunk = qdec[c_len - 1:c_len, :]
    state = jnp.zeros((RET_HEAD_DIM, RET_HEAD_DIM), F32)
    for c in range(s_len // c_len):
        rows = slice(c * c_len, (c + 1) * c_len)
        qc, kc, vc = q_ref[0, rows, :], k_ref[0, rows, :], v_ref[0, rows, :]
        inner = (_dot_nt(qc, kc) * decay).astype(BF16)
        y = _dot(inner, vc) + _dot(qc, state.astype(BF16)) * qdec
        kd = (kc.astype(F32) * kdec).astype(BF16)
        state = state * g_chunk + _dot_tn(kd, vc)
        mu = jnp.mean(y, axis=-1, keepdims=True)
        yc = y - mu
        var = jnp.mean(yc * yc, axis=-1, keepdims=True)
        o_ref[0, rows, :] = (_silu(g_ref[0, rows, :]) * (yc * lax.rsqrt(var + EPS))).astype(BF16)


def _retention_tables():
    c_len = RET_CHUNK
    gamma = 1.0 - jnp.power(2.0, -5.0 - jnp.arange(RET_HEADS, dtype=F32))
    log_g = jnp.log(gamma)
    idx = jnp.arange(c_len, dtype=F32)
    rel = idx[:, None] - idx[None, :]
    decay = jnp.where(rel[None] >= 0, jnp.exp(jnp.maximum(rel, 0.0)[None] * log_g[:, None, None]), 0.0)
    k_decay = jnp.exp((c_len - 1 - idx)[None, :] * log_g[:, None])
    q_decay = jnp.exp((idx + 1)[None, :] * log_g[:, None])
    shape = (RET_HEADS, c_len, LANES)
    return decay, jnp.broadcast_to(k_decay[:, :, None], shape), jnp.broadcast_to(q_decay[:, :, None], shape)


def _retention(rq, rk, rv, rg):
    b, s, _ = rq.shape
    spec = pl.BlockSpec((1, s, LANES), lambda i, h: (i, 0, h))
    tab = pl.BlockSpec((1, RET_CHUNK, LANES), lambda i, h: (h, 0, 0))
    return pl.pallas_call(
        _retention_kernel,
        grid=(b, RET_HEADS),
        in_specs=[spec] * 4 + [tab] * 3,
        out_specs=spec,
        out_shape=jax.ShapeDtypeStruct((b, s, RET_WIDTH), BF16),
        compiler_params=_params("arbitrary", "arbitrary"),
        name="retention",
    )(rq, rk, rv, rg, *_retention_tables())


def kernel(x, mem, even_mix_norm, even_w_in, pool_w, pool_scale, sgu_norm, sgu_w, sgu_b, even_w_out,
           odd_mix_norm, odd_w_in, odd_w_out, xattn_norm, mem_norm, xattn_wq, xattn_wkv, xattn_wo,
           ffn_norm, ffn_w_gate_up, ffn_w_down, final_norm):
    depth = xattn_wq.shape[0]
    assert x.shape[1] % (max(d for _, d in DIL_PAIRS) * DIL_BLOCK) == 0 and x.shape[2] == D_MODEL

    def row(v):
        return v.reshape(1, -1)

    kv = _memory_kv(mem, mem_norm, xattn_wkv.astype(BF16))
    h = x
    for layer in range(depth):
        i = layer // 2
        if layer % 2 == 0:
            h = _even_mixer(h, row(even_mix_norm[i]), even_w_in[i].astype(BF16), pool_w[i].astype(BF16),
                            row(pool_scale[i]), row(sgu_norm[i]), sgu_w[i], sgu_b[i].T,
                            even_w_out[i].astype(BF16))
            mix = None
        else:
            cq, ck, cv, rq, rk, rv, rg = _odd_proj(h, row(odd_mix_norm[i]), odd_w_in[i].astype(BF16))
            yc = _dilated_attention(cq, ck, cv)
            yd = _retention(rq, rk, rv, rg)
            mix = (yc, yd, odd_w_out[i].astype(BF16))
        h = _xattn(h, kv, layer, row(xattn_norm[layer]), xattn_wq[layer].astype(BF16),
                   xattn_wo[layer].astype(BF16), mix)
        h = _ffn(h, row(ffn_norm[layer]), ffn_w_gate_up[layer].astype(BF16), ffn_w_down[layer].astype(BF16),
                 row(final_norm), final_norm=layer == depth - 1)
    return h
```

```python
import functools
import math

import jax
import jax.numpy as jnp
from jax import lax
from jax.experimental import pallas as pl
from jax.experimental.pallas import tpu as pltpu

F32 = jnp.float32
BF16 = jnp.bfloat16

EPS = 1e-6
D_MODEL = 1024
N_MEM = 256

POOL_WINDOWS = (2, 4, 8, 16)
POOL_HALO = 16
POOL_WIDTH = 512
GROUP_DIM = 128
SGU_WIDTH = 512
SGU_CHUNK = 128
EVEN_IN = POOL_WIDTH + 2 * SGU_WIDTH

DIL_HEADS = 8
DIL_HEAD_DIM = 64
DIL_WIDTH = 512
DIL_PAIRS = ((128, 1), (512, 4), (2048, 16))
DIL_BLOCK = 128
DIL_UNROLL = 16
MASK_BIG = 2.0 ** 126
ROPE_THETA = 500000.0
ROPE_DIM = 16
RET_HEADS = 4
RET_HEAD_DIM = 128
RET_WIDTH = 512
RET_CHUNK = 128
RET_THETA = 10000.0
ODD_IN = 3 * DIL_WIDTH + 4 * RET_WIDTH

XATTN_HEADS = 4
XATTN_HEAD_DIM = 256
D_FF = 2816
FF_CHUNK = 1408

LANES = 128
SEQ_TILE = 512
VMEM_LIMIT = 56 * 1024 * 1024


def _params(*sem):
    return pltpu.CompilerParams(dimension_semantics=sem, vmem_limit_bytes=VMEM_LIMIT)


def _const_spec(shape):
    zeros = (0,) * len(shape)
    return pl.BlockSpec(shape, lambda *_: zeros, pipeline_mode=pl.Buffered(1))


def _rms(x, g):
    ms = jnp.mean(x * x, axis=-1, keepdims=True)
    return x * lax.rsqrt(ms + EPS) * g


def _gelu_tanh(x):
    return 0.5 * x * (1.0 + jnp.tanh(math.sqrt(2.0 / math.pi) * (x + 0.044715 * (x * x * x))))


def _silu(x):
    return x * (1.0 / (1.0 + jnp.exp(-x)))


def _dot(a, b):
    return jnp.dot(a, b, preferred_element_type=F32)


def _dot_nt(a, b):
    return lax.dot_general(a, b, (((1,), (1,)), ((), ())), preferred_element_type=F32)


def _dot_tn(a, b):
    return lax.dot_general(a, b, (((0,), (0,)), ((), ())), preferred_element_type=F32)


def _even_kernel(x_ref, g_ref, win_ref, pw_ref, ps_ref, sn_ref, sw_ref, sb_ref, wout_ref,
                 o_ref, ext_ref, ycat_ref):
    si = pl.program_id(1)
    ts = x_ref.shape[1]
    x = x_ref[0]
    hn = _rms(x, g_ref[...]).astype(BF16)

    a = _dot(hn, win_ref[:, :POOL_WIDTH])

    @pl.when(si == 0)
    def _():
        ext_ref[0:POOL_HALO, :] = jnp.zeros((POOL_HALO, POOL_WIDTH), F32)

    ext_ref[POOL_HALO:, :] = a
    pos = si * ts + lax.broadcasted_iota(jnp.int32, (ts, 1), 0)
    for g, win in enumerate(POOL_WINDOWS):
        cols = slice(g * GROUP_DIM, (g + 1) * GROUP_DIM)
        ag = a[:, cols]
        wsum = ag
        for j in range(1, win):
            wsum = wsum + ext_ref[POOL_HALO - j:POOL_HALO - j + ts, cols]
        cnt = jnp.minimum(pos + 1, win).astype(F32)
        d = (wsum / cnt - ag).astype(BF16)
        ya = _dot(d, pw_ref[g]) * ps_ref[:, cols]
        ycat_ref[:, cols] = ya.astype(BF16)
    ext_ref[0:POOL_HALO, :] = ext_ref[ts:ts + POOL_HALO, :]

    u = _gelu_tanh(_dot(hn, win_ref[:, POOL_WIDTH:POOL_WIDTH + SGU_WIDTH]))
    v = _gelu_tanh(_dot(hn, win_ref[:, POOL_WIDTH + SGU_WIDTH:]))
    vn = _rms(v, sn_ref[...]).astype(BF16)
    nc = ts // SGU_CHUNK
    row = lax.broadcasted_iota(jnp.int32, (SGU_CHUNK, SGU_CHUNK), 0)
    col = lax.broadcasted_iota(jnp.int32, (SGU_CHUNK, SGU_CHUNK), 1)
    for g in range(SGU_WIDTH // GROUP_DIM):
        cols = slice(g * GROUP_DIM, (g + 1) * GROUP_DIM)
        ws = jnp.where(row >= col, sw_ref[g], 0.0).astype(BF16)
        rhs = jnp.concatenate([vn[c * SGU_CHUNK:(c + 1) * SGU_CHUNK, cols] for c in range(nc)], axis=1)
        mixed = _dot(ws, rhs) + sb_ref[:, g:g + 1]
        for c in range(nc):
            rows = slice(c * SGU_CHUNK, (c + 1) * SGU_CHUNK)
            yb = u[rows, cols] * mixed[:, c * SGU_CHUNK:(c + 1) * SGU_CHUNK]
            ycat_ref[rows, POOL_WIDTH + g * GROUP_DIM:POOL_WIDTH + (g + 1) * GROUP_DIM] = yb.astype(BF16)

    o_ref[0] = x + _dot(ycat_ref[...], wout_ref[...])


def _even_mixer(h, g, w_in, pool_w, pool_scale, sgu_norm, sgu_w, sgu_b_t, w_out):
    b, s, d = h.shape
    ts = SEQ_TILE
    return pl.pallas_call(
        _even_kernel,
        grid=(b, s // ts),
        in_specs=[
            pl.BlockSpec((1, ts, d), lambda i, j: (i, j, 0)),
            _const_spec((1, d)),
            _const_spec((d, EVEN_IN)),
            _const_spec(pool_w.shape),
            _const_spec((1, POOL_WIDTH)),
            _const_spec((1, SGU_WIDTH)),
            _const_spec(sgu_w.shape),
            _const_spec(sgu_b_t.shape),
            _const_spec((POOL_WIDTH + SGU_WIDTH, d)),
        ],
        out_specs=pl.BlockSpec((1, ts, d), lambda i, j: (i, j, 0)),
        out_shape=jax.ShapeDtypeStruct(h.shape, F32),
        scratch_shapes=[
            pltpu.VMEM((POOL_HALO + ts, POOL_WIDTH), F32),
            pltpu.VMEM((ts, POOL_WIDTH + SGU_WIDTH), BF16),
        ],
        compiler_params=_params("arbitrary", "arbitrary"),
        name="even_mixer",
    )(h, g, w_in, pool_w, pool_scale, sgu_norm, sgu_w, sgu_b_t, w_out)


def _kv_kernel(m_ref, g_ref, w_ref, o_ref):
    mn = _rms(m_ref[0], g_ref[0]).astype(BF16)
    o_ref[0, 0] = _dot(mn, w_ref[0]).astype(BF16)


def _memory_kv(mem, mem_norm, wkv):
    b, m, d = mem.shape
    depth = wkv.shape[0]
    return pl.pallas_call(
        _kv_kernel,
        grid=(depth, b),
        in_specs=[
            pl.BlockSpec((1, m, d), lambda l, i: (i, 0, 0)),
            pl.BlockSpec((1, 1, d), lambda l, i: (l, 0, 0)),
            pl.BlockSpec((1, d, 2 * d), lambda l, i: (l, 0, 0)),
        ],
        out_specs=pl.BlockSpec((1, 1, m, 2 * d), lambda l, i: (l, i, 0, 0)),
        out_shape=jax.ShapeDtypeStruct((depth, b, m, 2 * d), BF16),
        compiler_params=_params("arbitrary", "arbitrary"),
        name="memory_kv",
    )(mem, mem_norm.reshape(depth, 1, d), wkv)


def _xattn_body(x, kv_ref, g_ref, wq_ref, wo_ref, o_ref, ocat_ref):
    hn = _rms(x, g_ref[...]).astype(BF16)
    q = (_dot(hn, wq_ref[...]) * (XATTN_HEAD_DIM ** -0.5)).astype(BF16)
    for h in range(XATTN_HEADS):
        cols = slice(h * XATTN_HEAD_DIM, (h + 1) * XATTN_HEAD_DIM)
        kh = kv_ref[0, 0, :, cols]
        vh = kv_ref[0, 0, :, D_MODEL + h * XATTN_HEAD_DIM:D_MODEL + (h + 1) * XATTN_HEAD_DIM]
        s = _dot_nt(q[:, cols], kh)
        p = jnp.exp(s - jnp.max(s, axis=-1, keepdims=True))
        l = jnp.sum(p, axis=-1, keepdims=True)
        ocat_ref[:, cols] = (_dot(p.astype(BF16), vh) * (1.0 / l)).astype(BF16)
    o_ref[0] = x + _dot(ocat_ref[...], wo_ref[...])


def _xattn_kernel(h_ref, kv_ref, g_ref, wq_ref, wo_ref, o_ref, ocat_ref):
    _xattn_body(h_ref[0], kv_ref, g_ref, wq_ref, wo_ref, o_ref, ocat_ref)


def _mix_xattn_kernel(h_ref, yc_ref, yd_ref, wmix_ref, kv_ref, g_ref, wq_ref, wo_ref, o_ref, ocat_ref):
    x = h_ref[0] + _dot(yc_ref[0], wmix_ref[:DIL_WIDTH, :]) + _dot(yd_ref[0], wmix_ref[DIL_WIDTH:, :])
    _xattn_body(x, kv_ref, g_ref, wq_ref, wo_ref, o_ref, ocat_ref)


def _xattn(h, kv, layer, g, wq, wo, mix=None):
    b, s, d = h.shape
    ts = SEQ_TILE
    tile = pl.BlockSpec((1, ts, d), lambda i, j: (i, j, 0))
    kv_spec = pl.BlockSpec((1, 1, N_MEM, 2 * d), lambda i, j: (layer, i, 0, 0))
    tail_specs = [kv_spec, _const_spec((1, d)), _const_spec((d, d)), _const_spec((d, d))]
    if mix is None:
        kern, in_specs, args = _xattn_kernel, [tile] + tail_specs, (h, kv, g, wq, wo)
    else:
        yc, yd, w_mix = mix
        half = pl.BlockSpec((1, ts, DIL_WIDTH), lambda i, j: (i, j, 0))
        kern = _mix_xattn_kernel
        in_specs = [tile, half, half, _const_spec((d, d))] + tail_specs
        args = (h, yc, yd, w_mix, kv, g, wq, wo)
    return pl.pallas_call(
        kern,
        grid=(b, s // ts),
        in_specs=in_specs,
        out_specs=tile,
        out_shape=jax.ShapeDtypeStruct(h.shape, F32),
        scratch_shapes=[pltpu.VMEM((ts, d), BF16)],
        compiler_params=_params("arbitrary", "arbitrary"),
        name="xattn" if mix is None else "mix_xattn",
    )(*args)


def _ffn_kernel(x_ref, g_ref, wgu_ref, wd_ref, fg_ref, o_ref, *, final_norm):
    x = x_ref[...]
    hn = _rms(x, g_ref[...]).astype(BF16)
    acc = x
    for c in range(D_FF // FF_CHUNK):
        gate = _dot(hn, wgu_ref[:, c * FF_CHUNK:(c + 1) * FF_CHUNK])
        up = _dot(hn, wgu_ref[:, D_FF + c * FF_CHUNK:D_FF + (c + 1) * FF_CHUNK])
        act = (_silu(gate) * up).astype(BF16)
        acc = acc + _dot(act, wd_ref[c * FF_CHUNK:(c + 1) * FF_CHUNK, :])
    if final_norm:
        acc = _rms(acc, fg_ref[...])
    o_ref[...] = acc


def _ffn(h, g, wgu, wd, fg, final_norm):
    b, s, d = h.shape
    t = b * s
    tm = SEQ_TILE
    tile = pl.BlockSpec((tm, d), lambda i: (i, 0))
    out = pl.pallas_call(
        functools.partial(_ffn_kernel, final_norm=final_norm),
        grid=(t // tm,),
        in_specs=[tile, _const_spec((1, d)), _const_spec((d, 2 * D_FF)), _const_spec((D_FF, d)),
                  _const_spec((1, d))],
        out_specs=tile,
        out_shape=jax.ShapeDtypeStruct((t, d), F32),
        compiler_params=_params("arbitrary"),
        name="ffn_final" if final_norm else "ffn",
    )(h.reshape(t, d), g, wgu, wd, fg)
    return out.reshape(b, s, d)


def _odd_proj_kernel(x_ref, g_ref, w_ref, qc_ref, qsa_ref, qsb_ref, dc_ref, dsa_ref, dsb_ref, rc_ref, rs_ref,
                     cq_ref, ck_ref, cv_ref, rq_ref, rk_ref, rv_ref, rg_ref):
    hn = _rms(x_ref[0], g_ref[...]).astype(BF16)

    def section(i):
        return _dot(hn, w_ref[:, i * DIL_WIDTH:(i + 1) * DIL_WIDTH])

    def lane_blocks(z, fn):
        return jnp.concatenate([fn(z[:, j * LANES:(j + 1) * LANES]) for j in range(z.shape[1] // LANES)], axis=1)

    half = ROPE_DIM // 2

    def dil_rot(c_ref, sa_ref, sb_ref):
        c, sa, sb = c_ref[...], sa_ref[...], sb_ref[...]
        return lambda zb: zb * c + pltpu.roll(zb, half, 1) * sa + pltpu.roll(zb, LANES - half, 1) * sb

    cq_ref[0] = lane_blocks(section(0), dil_rot(qc_ref, qsa_ref, qsb_ref))
    ck_ref[0] = lane_blocks(section(1), dil_rot(dc_ref, dsa_ref, dsb_ref))
    cv_ref[0] = section(2)

    rc, rs = rc_ref[...], rs_ref[...]

    def ret_rot(zb):
        return zb * rc + pltpu.roll(zb, RET_HEAD_DIM // 2, 1) * rs

    rq_ref[0] = lane_blocks(section(3), ret_rot).astype(BF16)
    rk_ref[0] = (lane_blocks(section(4), ret_rot) * (RET_HEAD_DIM ** -0.5)).astype(BF16)
    rv_ref[0] = section(5).astype(BF16)
    rg_ref[0] = section(6)


def _rotary_tables(s):
    pos = jnp.arange(s, dtype=jnp.int32).astype(F32)
    lane = jnp.arange(LANES)
    half = ROPE_DIM // 2
    inv = 1.0 / jnp.power(jnp.float32(ROPE_THETA), jnp.arange(half, dtype=F32) / half)
    ang = pos[:, None] * inv[None, :]
    cos, sin = jnp.cos(ang), jnp.sin(ang)
    hl = lane % DIL_HEAD_DIM
    cos_l, sin_l = cos[:, hl % half], sin[:, hl % half]
    dc = jnp.where(hl[None] < ROPE_DIM, cos_l, 1.0)
    dsa = jnp.where((hl[None] >= half) & (hl[None] < ROPE_DIM), sin_l, 0.0)
    dsb = jnp.where(hl[None] < half, -sin_l, 0.0)
    rhalf = RET_HEAD_DIM // 2
    rinv = 1.0 / jnp.power(jnp.float32(RET_THETA), jnp.arange(rhalf, dtype=F32) / rhalf)
    rang = pos[:, None] * rinv[None, :]
    rcos, rsin = jnp.cos(rang), jnp.sin(rang)
    rc = jnp.concatenate([rcos, rcos], axis=1)
    rs = jnp.concatenate([-rsin, rsin], axis=1)
    qs = (DIL_HEAD_DIM ** -0.5) * math.log2(math.e)
    return dc * qs, dsa * qs, dsb * qs, dc, dsa, dsb, rc, rs


def _odd_proj(h, g, w_in):
    b, s, d = h.shape
    ts = SEQ_TILE
    tables = _rotary_tables(s)
    tile = pl.BlockSpec((1, ts, d), lambda j, i: (i, j, 0))
    tab = pl.BlockSpec((ts, LANES), lambda j, i: (j, 0))
    out = pl.BlockSpec((1, ts, DIL_WIDTH), lambda j, i: (i, j, 0))
    f32o = jax.ShapeDtypeStruct((b, s, DIL_WIDTH), F32)
    bf16o = jax.ShapeDtypeStruct((b, s, RET_WIDTH), BF16)
    return pl.pallas_call(
        _odd_proj_kernel,
        grid=(s // ts, b),
        in_specs=[tile, _const_spec((1, d)), _const_spec((d, ODD_IN))] + [tab] * len(tables),
        out_specs=[out] * 7,
        out_shape=[f32o, f32o, f32o, bf16o, bf16o, bf16o, f32o],
        compiler_params=_params("arbitrary", "arbitrary"),
        name="odd_proj",
    )(h, g, w_in, *tables)


def _dilated_kernel(q_ref, k_ref, v_ref, o_ref, acc_ref, m_ref, l_ref, sel_ref, bias_ref):
    s_len = q_ref.shape[1]
    blk = DIL_BLOCK
    lane = lax.broadcasted_iota(jnp.int32, (blk, LANES), 1)
    first_head = lane < DIL_HEAD_DIM

    qi = lax.broadcasted_iota(jnp.int32, (2 * blk, blk), 1)
    kr = lax.broadcasted_iota(jnp.int32, (2 * blk, blk), 0)
    sel_ref[...] = jnp.where(kr % blk == qi, 1.0, 0.0).astype(BF16)
    bias_ref[0] = jnp.where(jnp.abs(qi + blk - kr - blk // 2) <= blk // 2, 0.0, -MASK_BIG).astype(BF16)
    bias_ref[1] = jnp.where(kr <= qi, 0.0, -MASK_BIG).astype(BF16)

    def rows(start, dil):
        return pl.ds(start, blk) if dil == 1 else pl.ds(start, blk, stride=dil)

    def block(branch, dil, start, with_prev):
        r_cur = rows(start, dil)
        q = q_ref[0, r_cur, :]
        q2 = jnp.concatenate([jnp.where(first_head, q, 0.0), jnp.where(first_head, 0.0, q)], axis=0).astype(BF16)
        kk = k_ref[0, r_cur, :].astype(BF16)
        vv = v_ref[0, r_cur, :].astype(BF16)
        if with_prev:
            r_prev = rows(start - dil * blk, dil)
            kk = jnp.concatenate([k_ref[0, r_prev, :].astype(BF16), kk], axis=0)
            vv = jnp.concatenate([v_ref[0, r_prev, :].astype(BF16), vv], axis=0)
            bias = bias_ref[0]
        else:
            bias = bias_ref[1, :blk, :]
        nk = kk.shape[0]
        s = _dot_nt(jnp.concatenate([q2, sel_ref[...]], axis=1), jnp.concatenate([kk, bias], axis=1))
        m = jnp.max(s, axis=-1, keepdims=True)
        p = jnp.exp2(s - m).astype(BF16)
        o2 = _dot(p, jnp.concatenate([vv, jnp.ones((nk, LANES), BF16)], axis=1))
        acc_ref[branch, r_cur, :] = jnp.where(first_head, o2[:blk, :LANES], o2[blk:, :LANES])
        l_ref[branch, r_cur, :] = jnp.where(first_head, o2[:blk, LANES:], o2[blk:, LANES:])
        m_ref[branch, r_cur, :] = jnp.where(first_head, m[:blk], m[blk:])

    for branch, (window, dil) in enumerate(DIL_PAIRS):
        assert window // dil == blk
        span = dil * blk
        nb = s_len // span
        if dil == 1:
            block(branch, dil, 0, False)
            unroll = max(u for u in range(1, DIL_UNROLL + 1) if (nb - 1) % u == 0)

            def step(i, c, branch=branch, dil=dil, unroll=unroll):
                for j in range(unroll):
                    block(branch, dil, pl.multiple_of((1 + i * unroll + j) * blk, blk), True)
                return c

            lax.fori_loop(0, (nb - 1) // unroll, step, 0)
        else:
            per_iter = max(1, DIL_UNROLL // nb)
            assert dil % per_iter == 0

            def residues(i, c, branch=branch, dil=dil, nb=nb, per_iter=per_iter, span=span):
                for j in range(per_iter):
                    for n in range(nb):
                        block(branch, dil, i * per_iter + j + n * span, n > 0)
                return c

            lax.fori_loop(0, dil // per_iter, residues, 0)

    m_all = jnp.maximum(jnp.maximum(m_ref[0], m_ref[1]), m_ref[2])
    num = jnp.zeros((s_len, LANES), F32)
    den = jnp.zeros((s_len, LANES), F32)
    for branch in range(len(DIL_PAIRS)):
        w = jnp.exp2(m_ref[branch] - m_all)
        num = num + w * acc_ref[branch]
        den = den + w * l_ref[branch]
    o_ref[0] = (num / den).astype(BF16)


def _dilated_attention(cq, ck, cv):
    b, s, _ = cq.shape
    spec = pl.BlockSpec((1, s, LANES), lambda i, p: (i, 0, p))
    nbr = len(DIL_PAIRS)
    return pl.pallas_call(
        _dilated_kernel,
        grid=(b, DIL_WIDTH // LANES),
        in_specs=[spec] * 3,
        out_specs=spec,
        out_shape=jax.ShapeDtypeStruct((b, s, DIL_WIDTH), BF16),
        scratch_shapes=[pltpu.VMEM((nbr, s, LANES), F32)] * 3 + [
            pltpu.VMEM((2 * DIL_BLOCK, LANES), BF16),
            pltpu.VMEM((2, 2 * DIL_BLOCK, LANES), BF16),
        ],
        compiler_params=_params("arbitrary", "arbitrary"),
        name="dilated_attention",
    )(cq, ck, cv)


def _retention_kernel(q_ref, k_ref, v_ref, g_ref, dec_ref, kdec_ref, qdec_ref, o_ref):
    s_len = q_ref.shape[1]
    c_len = RET_CHUNK
    decay = dec_ref[0]
    kdec = kdec_ref[0]
    qdec = qdec_ref[0]
    g_chunk = qdec[c_len - 1:c_len, :]
    state = jnp.zeros((RET_HEAD_DIM, RET_HEAD_DIM), F32)
    for c in range(s_len // c_len):
        rows = slice(c * c_len, (c + 1) * c_len)
        qc, kc, vc = q_ref[0, rows, :], k_ref[0, rows, :], v_ref[0, rows, :]
        inner = (_dot_nt(qc, kc) * decay).astype(BF16)
        y = _dot(inner, vc) + _dot(qc, state.astype(BF16)) * qdec
        kd = (kc.astype(F32) * kdec).astype(BF16)
        state = state * g_chunk + _dot_tn(kd, vc)
        mu = jnp.mean(y, axis=-1, keepdims=True)
        yc = y - mu
        var = jnp.mean(yc * yc, axis=-1, keepdims=True)
        o_ref[0, rows, :] = (_silu(g_ref[0, rows, :]) * (yc * lax.rsqrt(var + EPS))).astype(BF16)


def _retention_tables():
    c_len = RET_CHUNK
    gamma = 1.0 - jnp.power(2.0, -5.0 - jnp.arange(RET_HEADS, dtype=F32))
    log_g = jnp.log(gamma)
    idx = jnp.arange(c_len, dtype=F32)
    rel = idx[:, None] - idx[None, :]
    decay = jnp.where(rel[None] >= 0, jnp.exp(jnp.maximum(rel, 0.0)[None] * log_g[:, None, None]), 0.0)
    k_decay = jnp.exp((c_len - 1 - idx)[None, :] * log_g[:, None])
    q_decay = jnp.exp((idx + 1)[None, :] * log_g[:, None])
    shape = (RET_HEADS, c_len, LANES)
    return decay, jnp.broadcast_to(k_decay[:, :, None], shape), jnp.broadcast_to(q_decay[:, :, None], shape)


def _retention(rq, rk, rv, rg):
    b, s, _ = rq.shape
    spec = pl.BlockSpec((1, s, LANES), lambda i, h: (i, 0, h))
    tab = pl.BlockSpec((1, RET_CHUNK, LANES), lambda i, h: (h, 0, 0))
    return pl.pallas_call(
        _retention_kernel,
        grid=(b, RET_HEADS),
        in_specs=[spec] * 4 + [tab] * 3,
        out_specs=spec,
        out_shape=jax.ShapeDtypeStruct((b, s, RET_WIDTH), BF16),
        compiler_params=_params("arbitrary", "arbitrary"),
        name="retention",
    )(rq, rk, rv, rg, *_retention_tables())


def kernel(x, mem, even_mix_norm, even_w_in, pool_w, pool_scale, sgu_norm, sgu_w, sgu_b, even_w_out,
           odd_mix_norm, odd_w_in, odd_w_out, xattn_norm, mem_norm, xattn_wq, xattn_wkv, xattn_wo,
           ffn_norm, ffn_w_gate_up, ffn_w_down, final_norm):
    depth = xattn_wq.shape[0]
    assert x.shape[1] % (max(d for _, d in DIL_PAIRS) * DIL_BLOCK) == 0 and x.shape[2] == D_MODEL

    def row(v):
        return v.reshape(1, -1)

    kv = _memory_kv(mem, mem_norm, xattn_wkv.astype(BF16))
    h = x
    for layer in range(depth):
        i = layer // 2
        if layer % 2 == 0:
            h = _even_mixer(h, row(even_mix_norm[i]), even_w_in[i].astype(BF16), pool_w[i].astype(BF16),
                            row(pool_scale[i]), row(sgu_norm[i]), sgu_w[i], sgu_b[i].T,
                            even_w_out[i].astype(BF16))
            mix = None
        else:
            cq, ck, cv, rq, rk, rv, rg = _odd_proj(h, row(odd_mix_norm[i]), odd_w_in[i].astype(BF16))
            yc = _dilated_attention(cq, ck, cv)
            yd = _retention(rq, rk, rv, rg)
            mix = (yc, yd, odd_w_out[i].astype(BF16))
        h = _xattn(h, kv, layer, row(xattn_norm[layer]), xattn_wq[layer].astype(BF16),
                   xattn_wo[layer].astype(BF16), mix)
        h = _ffn(h, row(ffn_norm[layer]), ffn_w_gate_up[layer].astype(BF16), ffn_w_down[layer].astype(BF16),
                 row(final_norm), final_norm=layer == depth - 1)
    return h
```

```python
import functools
import math

import jax
import jax.numpy as jnp
from jax import lax
from jax.experimental import pallas as pl
from jax.experimental.pallas import tpu as pltpu

F32 = jnp.float32
BF16 = jnp.bfloat16

EPS = 1e-6
D_MODEL = 1024
N_MEM = 256

POOL_WINDOWS = (2, 4, 8, 16)
POOL_HALO = 8 * len(POOL_WINDOWS)
assert POOL_WINDOWS == tuple(2 ** (g + 1) for g in range(len(POOL_WINDOWS)))
POOL_WIDTH = 512
GROUP_DIM = 128
SGU_WIDTH = 512
SGU_CHUNK = 128
EVEN_IN = POOL_WIDTH + 2 * SGU_WIDTH

DIL_HEADS = 8
DIL_HEAD_DIM = 64
DIL_WIDTH = 512
DIL_PAIRS = ((128, 1), (512, 4), (2048, 16))
DIL_BLOCK = 128
DIL_UNROLL = 16
MASK_BIG = 2.0 ** 126
ROPE_THETA = 500000.0
ROPE_DIM = 16
RET_HEADS = 4
RET_HEAD_DIM = 128
RET_WIDTH = 512
RET_CHUNK = 256
RET_THETA = 10000.0
ODD_IN = 3 * DIL_WIDTH + 4 * RET_WIDTH

XATTN_HEADS = 4
XATTN_HEAD_DIM = 256
D_FF = 2816

LANES = 128
MXU_DIM = 256
FF_SPLIT = (D_FF // MXU_DIM + 1) // 2 * MXU_DIM
FF_CHUNKS = ((0, FF_SPLIT), (FF_SPLIT, D_FF))
SEQ_TILE = 512
VMEM_LIMIT = 56 * 1024 * 1024


def _params(*sem):
    return pltpu.CompilerParams(dimension_semantics=sem, vmem_limit_bytes=VMEM_LIMIT)


def _const_spec(shape):
    zeros = (0,) * len(shape)
    return pl.BlockSpec(shape, lambda *_: zeros, pipeline_mode=pl.Buffered(1))


def _rms(x, g):
    ms = jnp.mean(x * x, axis=-1, keepdims=True)
    return x * lax.rsqrt(ms + EPS) * g


def _gelu_tanh(x):
    return 0.5 * x * (1.0 + jnp.tanh(math.sqrt(2.0 / math.pi) * (x + 0.044715 * (x * x * x))))


def _silu(x):
    return x * (1.0 / (1.0 + jnp.exp(-x)))


def _dot(a, b):
    return jnp.dot(a, b, preferred_element_type=F32)


def _dot_nt(a, b):
    return lax.dot_general(a, b, (((1,), (1,)), ((), ())), preferred_element_type=F32)


def _even_kernel(x_ref, g_ref, win_ref, pw_ref, ps_ref, sn_ref, sw_ref, sb_ref, wout_ref,
                 o_ref, ext_ref, lva_ref, lvb_ref, ycat_ref):
    si = pl.program_id(1)
    ts = x_ref.shape[1]
    x = x_ref[0]
    hn = _rms(x, g_ref[...]).astype(BF16)

    a = _dot(hn, win_ref[:, :POOL_WIDTH])

    @pl.when(si == 0)
    def _():
        ext_ref[0:POOL_HALO, :] = jnp.zeros((POOL_HALO, POOL_WIDTH), F32)

    halo, end = POOL_HALO, POOL_HALO + ts
    ext_ref[halo:, :] = a
    src, bufs = ext_ref, (lva_ref, lvb_ref)
    for k in range(1, len(POOL_WINDOWS) + 1):
        dst, lo, back, c0 = bufs[(k - 1) % 2], 8 * k, 2 ** (k - 1), (k - 1) * GROUP_DIM
        dst[lo:end, c0:] = src[lo:end, c0:] + src[lo - back:end - back, c0:]
        src = dst
    ext_ref[0:halo, :] = ext_ref[ts:end, :]

    pos = si * ts + lax.broadcasted_iota(jnp.int32, (ts, 1), 0)
    for g, win in enumerate(POOL_WINDOWS):
        cols = slice(g * GROUP_DIM, (g + 1) * GROUP_DIM)
        wsum = bufs[g % 2][halo:end, cols]
        cnt = jnp.minimum(pos + 1, win).astype(F32)
        d = (wsum / cnt - a[:, cols]).astype(BF16)
        ya = _dot(d, pw_ref[g]) * ps_ref[:, cols]
        ycat_ref[:, cols] = ya.astype(BF16)

    u = _gelu_tanh(_dot(hn, win_ref[:, POOL_WIDTH:POOL_WIDTH + SGU_WIDTH]))
    v = _gelu_tanh(_dot(hn, win_ref[:, POOL_WIDTH + SGU_WIDTH:]))
    vn = _rms(v, sn_ref[...]).astype(BF16)
    nc = ts // SGU_CHUNK
    row = lax.broadcasted_iota(jnp.int32, (SGU_CHUNK, SGU_CHUNK), 0)
    col = lax.broadcasted_iota(jnp.int32, (SGU_CHUNK, SGU_CHUNK), 1)
    for g in range(SGU_WIDTH // GROUP_DIM):
        cols = slice(g * GROUP_DIM, (g + 1) * GROUP_DIM)
        ws = jnp.where(row >= col, sw_ref[g], 0.0).astype(BF16)
        rhs = jnp.concatenate([vn[c * SGU_CHUNK:(c + 1) * SGU_CHUNK, cols] for c in range(nc)], axis=1)
        mixed = _dot(ws, rhs) + sb_ref[:, g:g + 1]
        for c in range(nc):
            rows = slice(c * SGU_CHUNK, (c + 1) * SGU_CHUNK)
            yb = u[rows, cols] * mixed[:, c * SGU_CHUNK:(c + 1) * SGU_CHUNK]
            ycat_ref[rows, POOL_WIDTH + g * GROUP_DIM:POOL_WIDTH + (g + 1) * GROUP_DIM] = yb.astype(BF16)

    o_ref[0] = x + _dot(ycat_ref[...], wout_ref[...])


def _even_mixer(h, g, w_in, pool_w, pool_scale, sgu_norm, sgu_w, sgu_b_t, w_out):
    b, s, d = h.shape
    ts = SEQ_TILE
    return pl.pallas_call(
        _even_kernel,
        grid=(b, s // ts),
        in_specs=[
            pl.BlockSpec((1, ts, d), lambda i, j: (i, j, 0)),
            _const_spec((1, d)),
            _const_spec((d, EVEN_IN)),
            _const_spec(pool_w.shape),
            _const_spec((1, POOL_WIDTH)),
            _const_spec((1, SGU_WIDTH)),
            _const_spec(sgu_w.shape),
            _const_spec(sgu_b_t.shape),
            _const_spec((POOL_WIDTH + SGU_WIDTH, d)),
        ],
        out_specs=pl.BlockSpec((1, ts, d), lambda i, j: (i, j, 0)),
        out_shape=jax.ShapeDtypeStruct(h.shape, F32),
        scratch_shapes=[pltpu.VMEM((POOL_HALO + ts, POOL_WIDTH), F32)] * 3 + [
            pltpu.VMEM((ts, POOL_WIDTH + SGU_WIDTH), BF16),
        ],
        compiler_params=_params("arbitrary", "arbitrary"),
        name="even_mixer",
    )(h, g, w_in, pool_w, pool_scale, sgu_norm, sgu_w, sgu_b_t, w_out)


def _kv_kernel(m_ref, g_ref, w_ref, o_ref):
    mn = _rms(m_ref[0], g_ref[0]).astype(BF16)
    o_ref[0, 0] = _dot(mn, w_ref[0]).astype(BF16)


def _memory_kv(mem, mem_norm, wkv):
    b, m, d = mem.shape
    depth = wkv.shape[0]
    return pl.pallas_call(
        _kv_kernel,
        grid=(depth, b),
        in_specs=[
            pl.BlockSpec((1, m, d), lambda l, i: (i, 0, 0)),
            pl.BlockSpec((1, 1, d), lambda l, i: (l, 0, 0)),
            pl.BlockSpec((1, d, 2 * d), lambda l, i: (l, 0, 0)),
        ],
        out_specs=pl.BlockSpec((1, 1, m, 2 * d), lambda l, i: (l, i, 0, 0)),
        out_shape=jax.ShapeDtypeStruct((depth, b, m, 2 * d), BF16),
        compiler_params=_params("arbitrary", "arbitrary"),
        name="memory_kv",
    )(mem, mem_norm.reshape(depth, 1, d), wkv)


def _xattn_body(x, kv_ref, g_ref, wq_ref, wo_ref, o_ref, ocat_ref):
    hn = _rms(x, g_ref[...]).astype(BF16)
    q = (_dot(hn, wq_ref[...]) * (XATTN_HEAD_DIM ** -0.5)).astype(BF16)
    for h in range(XATTN_HEADS):
        cols = slice(h * XATTN_HEAD_DIM, (h + 1) * XATTN_HEAD_DIM)
        kh = kv_ref[0, 0, :, cols]
        vh = kv_ref[0, 0, :, D_MODEL + h * XATTN_HEAD_DIM:D_MODEL + (h + 1) * XATTN_HEAD_DIM]
        s = _dot_nt(q[:, cols], kh)
        p = jnp.exp(s - jnp.max(s, axis=-1, keepdims=True))
        l = jnp.sum(p, axis=-1, keepdims=True)
        ocat_ref[:, cols] = (_dot(p.astype(BF16), vh) * (1.0 / l)).astype(BF16)
    o_ref[0] = x + _dot(ocat_ref[...], wo_ref[...])


def _xattn_kernel(h_ref, kv_ref, g_ref, wq_ref, wo_ref, o_ref, ocat_ref):
    _xattn_body(h_ref[0], kv_ref, g_ref, wq_ref, wo_ref, o_ref, ocat_ref)


def _mix_xattn_kernel(h_ref, yc_ref, yd_ref, wmix_ref, kv_ref, g_ref, wq_ref, wo_ref, o_ref, ocat_ref):
    x = h_ref[0] + _dot(yc_ref[0], wmix_ref[:DIL_WIDTH, :]) + _dot(yd_ref[0], wmix_ref[DIL_WIDTH:, :])
    _xattn_body(x, kv_ref, g_ref, wq_ref, wo_ref, o_ref, ocat_ref)


def _xattn(h, kv, layer, g, wq, wo, mix=None):
    b, s, d = h.shape
    ts = SEQ_TILE
    tile = pl.BlockSpec((1, ts, d), lambda i, j: (i, j, 0))
    kv_spec = pl.BlockSpec((1, 1, N_MEM, 2 * d), lambda i, j: (layer, i, 0, 0))
    tail_specs = [kv_spec, _const_spec((1, d)), _const_spec((d, d)), _const_spec((d, d))]
    if mix is None:
        kern, in_specs, args = _xattn_kernel, [tile] + tail_specs, (h, kv, g, wq, wo)
    else:
        yc, yd, w_mix = mix
        half = pl.BlockSpec((1, ts, DIL_WIDTH), lambda i, j: (i, j, 0))
        kern = _mix_xattn_kernel
        in_specs = [tile, half, half, _const_spec((d, d))] + tail_specs
        args = (h, yc, yd, w_mix, kv, g, wq, wo)
    return pl.pallas_call(
        kern,
        grid=(b, s // ts),
        in_specs=in_specs,
        out_specs=tile,
        out_shape=jax.ShapeDtypeStruct(h.shape, F32),
        scratch_shapes=[pltpu.VMEM((ts, d), BF16)],
        compiler_params=_params("arbitrary", "arbitrary"),
        name="xattn" if mix is None else "mix_xattn",
    )(*args)


def _ffn_kernel(x_ref, g_ref, wgu_ref, wd_ref, fg_ref, o_ref, *, final_norm):
    x = x_ref[...]
    hn = _rms(x, g_ref[...]).astype(BF16)
    acc = x
    for lo, hi in FF_CHUNKS:
        gate = _dot(hn, wgu_ref[:, lo:hi])
        up = _dot(hn, wgu_ref[:, D_FF + lo:D_FF + hi])
        act = (_silu(gate) * up).astype(BF16)
        acc = acc + _dot(act, wd_ref[lo:hi, :])
    if final_norm:
        acc = _rms(acc, fg_ref[...])
    o_ref[...] = acc


def _ffn(h, g, wgu, wd, fg, final_norm):
    b, s, d = h.shape
    t = b * s
    tm = SEQ_TILE
    tile = pl.BlockSpec((tm, d), lambda i: (i, 0))
    out = pl.pallas_call(
        functools.partial(_ffn_kernel, final_norm=final_norm),
        grid=(t // tm,),
        in_specs=[tile, _const_spec((1, d)), _const_spec((d, 2 * D_FF)), _const_spec((D_FF, d)),
                  _const_spec((1, d))],
        out_specs=tile,
        out_shape=jax.ShapeDtypeStruct((t, d), F32),
        compiler_params=_params("arbitrary"),
        name="ffn_final" if final_norm else "ffn",
    )(h.reshape(t, d), g, wgu, wd, fg)
    return out.reshape(b, s, d)


def _odd_proj_kernel(x_ref, g_ref, w_ref, wkt_ref, qc_ref, qsa_ref, qsb_ref, dc_ref, dsa_ref, dsb_ref,
                     rc_ref, rs_ref, kct_ref, kst_ref,
                     cq_ref, ck_ref, cv_ref, rq_ref, rkt_ref, rv_ref, rg_ref):
    hn = _rms(x_ref[0], g_ref[...]).astype(BF16)

    def section(i):
        return _dot(hn, w_ref[:, i * DIL_WIDTH:(i + 1) * DIL_WIDTH])

    def lane_blocks(z, fn):
        return jnp.concatenate([fn(z[:, j * LANES:(j + 1) * LANES]) for j in range(z.shape[1] // LANES)], axis=1)

    half = ROPE_DIM // 2

    def dil_rot(c_ref, sa_ref, sb_ref):
        c, sa, sb = c_ref[...], sa_ref[...], sb_ref[...]
        return lambda zb: zb * c + pltpu.roll(zb, half, 1) * sa + pltpu.roll(zb, LANES - half, 1) * sb

    cq_ref[0] = lane_blocks(section(0), dil_rot(qc_ref, qsa_ref, qsb_ref))
    ck_ref[0] = lane_blocks(section(1), dil_rot(dc_ref, dsa_ref, dsb_ref))
    cv_ref[0] = section(2)

    rc, rs = rc_ref[...], rs_ref[...]

    def ret_rot(zb):
        return zb * rc + pltpu.roll(zb, RET_HEAD_DIM // 2, 1) * rs

    rq_ref[0] = lane_blocks(section(3), ret_rot).astype(BF16)
    rv_ref[0] = section(5).astype(BF16)
    rg_ref[0] = section(6)

    kt = _dot_nt(wkt_ref[...], hn)
    kct, kst = kct_ref[...], kst_ref[...]
    hd = RET_HEAD_DIM
    for h in range(RET_HEADS):
        zb = kt[h * hd:(h + 1) * hd, :]
        swapped = jnp.concatenate([zb[hd // 2:], zb[:hd // 2]], axis=0)
        rkt_ref[0, h * hd:(h + 1) * hd, :] = (zb * kct + swapped * kst).astype(BF16)


def _rotary_tables(s):
    pos = jnp.arange(s, dtype=jnp.int32).astype(F32)
    lane = jnp.arange(LANES)
    half = ROPE_DIM // 2
    inv = 1.0 / jnp.power(jnp.float32(ROPE_THETA), jnp.arange(half, dtype=F32) / half)
    ang = pos[:, None] * inv[None, :]
    cos, sin = jnp.cos(ang), jnp.sin(ang)
    hl = lane % DIL_HEAD_DIM
    cos_l, sin_l = cos[:, hl % half], sin[:, hl % half]
    dc = jnp.where(hl[None] < ROPE_DIM, cos_l, 1.0)
    dsa = jnp.where((hl[None] >= half) & (hl[None] < ROPE_DIM), sin_l, 0.0)
    dsb = jnp.where(hl[None] < half, -sin_l, 0.0)
    rhalf = RET_HEAD_DIM // 2
    rinv = 1.0 / jnp.power(jnp.float32(RET_THETA), jnp.arange(rhalf, dtype=F32) / rhalf)
    rang = pos[:, None] * rinv[None, :]
    rcos, rsin = jnp.cos(rang), jnp.sin(rang)
    rc = jnp.concatenate([rcos, rcos], axis=1)
    rs = jnp.concatenate([-rsin, rsin], axis=1)
    qs = (DIL_HEAD_DIM ** -0.5) * math.log2(math.e)
    ks = RET_HEAD_DIM ** -0.5
    return (dc * qs, dsa * qs, dsb * qs, dc, dsa, dsb, rc, rs), ((rc * ks).T, (rs * ks).T)


def _odd_proj(h, g, w_in):
    b, s, d = h.shape
    ts = SEQ_TILE
    tables, tables_t = _rotary_tables(s)
    wkt = w_in[:, 3 * DIL_WIDTH + RET_WIDTH:3 * DIL_WIDTH + 2 * RET_WIDTH].T
    tile = pl.BlockSpec((1, ts, d), lambda j, i: (i, j, 0))
    tab = pl.BlockSpec((ts, LANES), lambda j, i: (j, 0))
    tab_t = pl.BlockSpec((LANES, ts), lambda j, i: (0, j))
    out = pl.BlockSpec((1, ts, DIL_WIDTH), lambda j, i: (i, j, 0))
    out_t = pl.BlockSpec((1, RET_WIDTH, ts), lambda j, i: (i, 0, j))
    f32o = jax.ShapeDtypeStruct((b, s, DIL_WIDTH), F32)
    bf16o = jax.ShapeDtypeStruct((b, s, RET_WIDTH), BF16)
    return pl.pallas_call(
        _odd_proj_kernel,
        grid=(s // ts, b),
        in_specs=[tile, _const_spec((1, d)), _const_spec((d, ODD_IN)), _const_spec((RET_WIDTH, d))]
        + [tab] * len(tables) + [tab_t] * len(tables_t),
        out_specs=[out, out, out, out, out_t, out, out],
        out_shape=[f32o, f32o, f32o, bf16o, jax.ShapeDtypeStruct((b, RET_WIDTH, s), BF16), bf16o, f32o],
        compiler_params=_params("arbitrary", "arbitrary"),
        name="odd_proj",
    )(h, g, w_in, wkt, *tables, *tables_t)


def _dilated_kernel(q_ref, k_ref, v_ref, o_ref, acc_ref, m_ref, l_ref, sel_ref, bias_ref):
    s_len = q_ref.shape[1]
    blk = DIL_BLOCK
    lane = lax.broadcasted_iota(jnp.int32, (blk, LANES), 1)
    first_head = lane < DIL_HEAD_DIM

    qi = lax.broadcasted_iota(jnp.int32, (2 * blk, blk), 1)
    kr = lax.broadcasted_iota(jnp.int32, (2 * blk, blk), 0)
    sel_ref[...] = jnp.where(kr % blk == qi, 1.0, 0.0).astype(BF16)
    bias_ref[0] = jnp.where(jnp.abs(qi + blk - kr - blk // 2) <= blk // 2, 0.0, -MASK_BIG).astype(BF16)
    bias_ref[1] = jnp.where(kr <= qi, 0.0, -MASK_BIG).astype(BF16)

    def rows(start, dil):
        return pl.ds(start, blk) if dil == 1 else pl.ds(start, blk, stride=dil)

    def block(branch, dil, start, with_prev):
        r_cur = rows(start, dil)
        q = q_ref[0, r_cur, :]
        q2 = jnp.concatenate([jnp.where(first_head, q, 0.0), jnp.where(first_head, 0.0, q)], axis=0).astype(BF16)
        kk = k_ref[0, r_cur, :].astype(BF16)
        vv = v_ref[0, r_cur, :].astype(BF16)
        if with_prev:
            r_prev = rows(start - dil * blk, dil)
            kk = jnp.concatenate([k_ref[0, r_prev, :].astype(BF16), kk], axis=0)
            vv = jnp.concatenate([v_ref[0, r_prev, :].astype(BF16), vv], axis=0)
            bias = bias_ref[0]
        else:
            bias = bias_ref[1, :blk, :]
        nk = kk.shape[0]
        s = _dot_nt(jnp.concatenate([q2, sel_ref[...]], axis=1), jnp.concatenate([kk, bias], axis=1))
        m = jnp.max(s, axis=-1, keepdims=True)
        p = jnp.exp2(s - m).astype(BF16)
        o2 = _dot(p, jnp.concatenate([vv, jnp.ones((nk, LANES), BF16)], axis=1))
        acc_ref[branch, r_cur, :] = jnp.where(first_head, o2[:blk, :LANES], o2[blk:, :LANES])
        l_ref[branch, r_cur, :] = jnp.where(first_head, o2[:blk, LANES:], o2[blk:, LANES:])
        m_ref[branch, r_cur, :] = jnp.where(first_head, m[:blk], m[blk:])

    for branch, (window, dil) in enumerate(DIL_PAIRS):
        assert window // dil == blk
        span = dil * blk
        nb = s_len // span
        if dil == 1:
            block(branch, dil, 0, False)
            unroll = max(u for u in range(1, DIL_UNROLL + 1) if (nb - 1) % u == 0)

            def step(i, c, branch=branch, dil=dil, unroll=unroll):
                for j in range(unroll):
                    block(branch, dil, pl.multiple_of((1 + i * unroll + j) * blk, blk), True)
                return c

            lax.fori_loop(0, (nb - 1) // unroll, step, 0)
        else:
            per_iter = max(1, DIL_UNROLL // nb)
            assert dil % per_iter == 0

            def residues(i, c, branch=branch, dil=dil, nb=nb, per_iter=per_iter, span=span):
                for j in range(per_iter):
                    for n in range(nb):
                        block(branch, dil, i * per_iter + j + n * span, n > 0)
                return c

            lax.fori_loop(0, dil // per_iter, residues, 0)

    m_all = jnp.maximum(jnp.maximum(m_ref[0], m_ref[1]), m_ref[2])
    num = jnp.zeros((s_len, LANES), F32)
    den = jnp.zeros((s_len, LANES), F32)
    for branch in range(len(DIL_PAIRS)):
        w = jnp.exp2(m_ref[branch] - m_all)
        num = num + w * acc_ref[branch]
        den = den + w * l_ref[branch]
    o_ref[0] = (num / den).astype(BF16)


def _dilated_attention(cq, ck, cv):
    b, s, _ = cq.shape
    spec = pl.BlockSpec((1, s, LANES), lambda i, p: (i, 0, p))
    nbr = len(DIL_PAIRS)
    return pl.pallas_call(
        _dilated_kernel,
        grid=(b, DIL_WIDTH // LANES),
        in_specs=[spec] * 3,
        out_specs=spec,
        out_shape=jax.ShapeDtypeStruct((b, s, DIL_WIDTH), BF16),
        scratch_shapes=[pltpu.VMEM((nbr, s, LANES), F32)] * 3 + [
            pltpu.VMEM((2 * DIL_BLOCK, LANES), BF16),
            pltpu.VMEM((2, 2 * DIL_BLOCK, LANES), BF16),
        ],
        compiler_params=_params("arbitrary", "arbitrary"),
        name="dilated_attention",
    )(cq, ck, cv)


def _retention_kernel(q_ref, kt_ref, v_ref, g_ref, dec_ref, kdec_ref, qdec_ref, o_ref,
                      y_ref, kv_ref, st_ref, s_ref, in_ref):
    s_len = q_ref.shape[1]
    c_len = RET_CHUNK
    n_chunks = s_len // c_len
    decay = dec_ref[0]
    kdec = kdec_ref[0]
    qdec = qdec_ref[0]
    g_chunk = qdec[c_len - 1:c_len, :]

    def chunk(c):
        return slice(c * c_len, (c + 1) * c_len)

    for c in range(n_chunks):
        ktc = kt_ref[0, :, chunk(c)]
        s_ref[c] = _dot(q_ref[0, chunk(c), :], ktc)
        kv_ref[c] = _dot((ktc.astype(F32) * kdec).astype(BF16), v_ref[0, chunk(c), :])
    for c in range(n_chunks):
        in_ref[c] = (s_ref[c] * decay).astype(BF16)
    for c in range(n_chunks):
        y_ref[chunk(c), :] = _dot(in_ref[c], v_ref[0, chunk(c), :])
    state = jnp.zeros((RET_HEAD_DIM, RET_HEAD_DIM), F32)
    for c in range(n_chunks):
        st_ref[c] = state.astype(BF16)
        state = state * g_chunk + kv_ref[c]
    for c in range(n_chunks):
        y = y_ref[chunk(c), :] + _dot(q_ref[0, chunk(c), :], st_ref[c]) * qdec
        mu = jnp.mean(y, axis=-1, keepdims=True)
        yc = y - mu
        var = jnp.mean(yc * yc, axis=-1, keepdims=True)
        o_ref[0, chunk(c), :] = (_silu(g_ref[0, chunk(c), :]) * (yc * lax.rsqrt(var + EPS))).astype(BF16)


def _retention_tables():
    c_len = RET_CHUNK
    gamma = 1.0 - jnp.power(2.0, -5.0 - jnp.arange(RET_HEADS, dtype=F32))
    log_g = jnp.log(gamma)
    idx = jnp.arange(c_len, dtype=F32)
    rel = idx[:, None] - idx[None, :]
    decay = jnp.where(rel[None] >= 0, jnp.exp(jnp.maximum(rel, 0.0)[None] * log_g[:, None, None]), 0.0)
    k_decay = jnp.exp((c_len - 1 - idx)[None, :] * log_g[:, None])
    q_decay = jnp.exp((idx + 1)[None, :] * log_g[:, None])
    return (decay, jnp.broadcast_to(k_decay[:, None, :], (RET_HEADS, RET_HEAD_DIM, c_len)),
            jnp.broadcast_to(q_decay[:, :, None], (RET_HEADS, c_len, RET_HEAD_DIM)))


def _retention(rq, rkt, rv, rg):
    b, s, _ = rq.shape
    hd = RET_HEAD_DIM
    spec = pl.BlockSpec((1, s, hd), lambda i, h: (i, 0, h))
    spec_t = pl.BlockSpec((1, hd, s), lambda i, h: (i, h, 0))
    tables = _retention_tables()
    return pl.pallas_call(
        _retention_kernel,
        grid=(b, RET_HEADS),
        in_specs=[spec, spec_t, spec, spec]
        + [pl.BlockSpec((1,) + t.shape[1:], lambda i, h: (h, 0, 0)) for t in tables],
        out_specs=spec,
        out_shape=jax.ShapeDtypeStruct((b, s, RET_WIDTH), BF16),
        scratch_shapes=[
            pltpu.VMEM((s, hd), F32),
            pltpu.VMEM((s // RET_CHUNK, hd, hd), F32),
            pltpu.VMEM((s // RET_CHUNK, hd, hd), BF16),
            pltpu.VMEM((s // RET_CHUNK, RET_CHUNK, RET_CHUNK), F32),
            pltpu.VMEM((s // RET_CHUNK, RET_CHUNK, RET_CHUNK), BF16),
        ],
        compiler_params=_params("arbitrary", "arbitrary"),
        name="retention",
    )(rq, rkt, rv, rg, *tables)


def kernel(x, mem, even_mix_norm, even_w_in, pool_w, pool_scale, sgu_norm, sgu_w, sgu_b, even_w_out,
           odd_mix_norm, odd_w_in, odd_w_out, xattn_norm, mem_norm, xattn_wq, xattn_wkv, xattn_wo,
           ffn_norm, ffn_w_gate_up, ffn_w_down, final_norm):
    depth = xattn_wq.shape[0]
    assert x.shape[1] % (max(d for _, d in DIL_PAIRS) * DIL_BLOCK) == 0 and x.shape[2] == D_MODEL

    def row(v):
        return v.reshape(1, -1)

    kv = _memory_kv(mem, mem_norm, xattn_wkv.astype(BF16))
    h = x
    for layer in range(depth):
        i = layer // 2
        if layer % 2 == 0:
            h = _even_mixer(h, row(even_mix_norm[i]), even_w_in[i].astype(BF16), pool_w[i].astype(BF16),
                            row(pool_scale[i]), row(sgu_norm[i]), sgu_w[i], sgu_b[i].T,
                            even_w_out[i].astype(BF16))
            mix = None
        else:
            cq, ck, cv, rq, rk, rv, rg = _odd_proj(h, row(odd_mix_norm[i]), odd_w_in[i].astype(BF16))
            yc = _dilated_attention(cq, ck, cv)
            yd = _retention(rq, rk, rv, rg)
            mix = (yc, yd, odd_w_out[i].astype(BF16))
        h = _xattn(h, kv, layer, row(xattn_norm[layer]), xattn_wq[layer].astype(BF16),
                   xattn_wo[layer].astype(BF16), mix)
        h = _ffn(h, row(ffn_norm[layer]), ffn_w_gate_up[layer].astype(BF16), ffn_w_down[layer].astype(BF16),
                 row(final_norm), final_norm=layer == depth - 1)
    return h
```

```python
import functools
import math

import jax
import jax.numpy as jnp
from jax import lax
from jax.experimental import pallas as pl
from jax.experimental.pallas import tpu as pltpu

F32 = jnp.float32
BF16 = jnp.bfloat16

EPS = 1e-6
D_MODEL = 1024
N_MEM = 256

POOL_WINDOWS = (2, 4, 8, 16)
POOL_HALO = 8 * len(POOL_WINDOWS)
assert POOL_WINDOWS == tuple(2 ** (g + 1) for g in range(len(POOL_WINDOWS)))
POOL_WIDTH = 512
GROUP_DIM = 128
SGU_WIDTH = 512
SGU_CHUNK = 128
EVEN_IN = POOL_WIDTH + 2 * SGU_WIDTH

DIL_HEADS = 8
DIL_HEAD_DIM = 64
DIL_WIDTH = 512
DIL_PAIRS = ((128, 1), (512, 4), (2048, 16))
DIL_BLOCK = 128
DIL_SLOTS = 3
MASK_BIG = 2.0 ** 126
ROPE_THETA = 500000.0
ROPE_DIM = 16
RET_HEADS = 4
RET_HEAD_DIM = 128
RET_WIDTH = 512
RET_CHUNK = 256
RET_THETA = 10000.0
ODD_IN = 3 * DIL_WIDTH + 4 * RET_WIDTH

XATTN_HEADS = 4
XATTN_HEAD_DIM = 256
D_FF = 2816

LANES = 128
MXU_DIM = 256
FF_SPLIT = (D_FF // MXU_DIM + 1) // 2 * MXU_DIM
FF_CHUNKS = ((0, FF_SPLIT), (FF_SPLIT, D_FF))
SEQ_TILE = 512
VMEM_LIMIT = 56 * 1024 * 1024


def _params(*sem):
    return pltpu.CompilerParams(dimension_semantics=sem, vmem_limit_bytes=VMEM_LIMIT)


def _const_spec(shape):
    zeros = (0,) * len(shape)
    return pl.BlockSpec(shape, lambda *_: zeros, pipeline_mode=pl.Buffered(1))


def _rms(x, g):
    ms = jnp.mean(x * x, axis=-1, keepdims=True)
    return x * lax.rsqrt(ms + EPS) * g


def _gelu_tanh(x):
    return 0.5 * x * (1.0 + jnp.tanh(math.sqrt(2.0 / math.pi) * (x + 0.044715 * (x * x * x))))


def _silu(x):
    return x * (1.0 / (1.0 + jnp.exp(-x)))


def _dot(a, b):
    return jnp.dot(a, b, preferred_element_type=F32)


def _dot_nt(a, b):
    return lax.dot_general(a, b, (((1,), (1,)), ((), ())), preferred_element_type=F32)


def _even_kernel(x_ref, g_ref, win_ref, pw_ref, ps_ref, sn_ref, sw_ref, sb_ref, wout_ref,
                 o_ref, ext_ref, lva_ref, lvb_ref, u_ref, v_ref, ycat_ref):
    si = pl.program_id(1)
    ts = x_ref.shape[1]
    halo, end = POOL_HALO, POOL_HALO + ts
    half = MXU_DIM
    x = x_ref[0]

    @pl.when(si == 0)
    def _():
        ext_ref[0:halo, :] = jnp.zeros((halo, POOL_WIDTH), F32)

    hn_rows = []
    for r in range(0, ts, ts // 2):
        hn_rows.append(_rms(x[r:r + ts // 2], g_ref[...]).astype(BF16))
        ext_ref[halo + r:halo + r + ts // 2, :] = _dot(hn_rows[-1], win_ref[:, :POOL_WIDTH])
    hn = jnp.concatenate(hn_rows, axis=0)

    def in_proj(ref, c0, c1):
        ref[:, c0 - c1:c0 - c1 + half] = _dot(hn, win_ref[:, c0:c0 + half])

    def gelu_inplace(ref, c):
        ref[:, c:c + half] = _gelu_tanh(ref[:, c:c + half])

    bufs = (lva_ref, lvb_ref)

    def level(k):
        src = ext_ref if k == 1 else bufs[k % 2]
        dst, lo, back, c0 = bufs[(k - 1) % 2], 8 * k, 2 ** (k - 1), (k - 1) * GROUP_DIM
        dst[lo:end, c0:] = src[lo:end, c0:] + src[lo - back:end - back, c0:]

    u0, v0 = POOL_WIDTH, POOL_WIDTH + SGU_WIDTH
    in_proj(u_ref, u0, u0)
    level(1)
    level(2)
    in_proj(u_ref, u0 + half, u0)
    level(3)
    level(4)
    assert len(POOL_WINDOWS) == 4

    in_proj(v_ref, v0, v0)
    pos = si * ts + lax.broadcasted_iota(jnp.int32, (ts, 1), 0)
    for g, win in enumerate(POOL_WINDOWS):
        cols = slice(g * GROUP_DIM, (g + 1) * GROUP_DIM)
        wsum = bufs[g % 2][halo:end, cols]
        cnt = jnp.minimum(pos + 1, win).astype(F32)
        d = (wsum / cnt - ext_ref[halo:end, cols]).astype(BF16)
        ya = _dot(d, pw_ref[g]) * ps_ref[:, cols]
        ycat_ref[:, cols] = ya.astype(BF16)
    ext_ref[0:halo, :] = ext_ref[ts:end, :]

    in_proj(v_ref, v0 + half, v0)
    gelu_inplace(u_ref, 0)
    gelu_inplace(u_ref, half)

    acc = x + _dot(ycat_ref[:, :POOL_WIDTH], wout_ref[:POOL_WIDTH, :])
    gelu_inplace(v_ref, 0)
    gelu_inplace(v_ref, half)
    vn = _rms(v_ref[...], sn_ref[...]).astype(BF16)

    nc = ts // SGU_CHUNK
    row = lax.broadcasted_iota(jnp.int32, (SGU_CHUNK, SGU_CHUNK), 0)
    col = lax.broadcasted_iota(jnp.int32, (SGU_CHUNK, SGU_CHUNK), 1)
    groups_per_tile = MXU_DIM // GROUP_DIM
    for g in range(SGU_WIDTH // GROUP_DIM):
        cols = slice(g * GROUP_DIM, (g + 1) * GROUP_DIM)
        ws = jnp.where(row >= col, sw_ref[g], 0.0).astype(BF16)
        rhs = jnp.concatenate([vn[c * SGU_CHUNK:(c + 1) * SGU_CHUNK, cols] for c in range(nc)], axis=1)
        mixed = _dot(ws, rhs) + sb_ref[:, g:g + 1]
        for c in range(nc):
            rows = slice(c * SGU_CHUNK, (c + 1) * SGU_CHUNK)
            yb = u_ref[rows, cols] * mixed[:, c * SGU_CHUNK:(c + 1) * SGU_CHUNK]
            ycat_ref[rows, POOL_WIDTH + g * GROUP_DIM:POOL_WIDTH + (g + 1) * GROUP_DIM] = yb.astype(BF16)
        if (g + 1) % groups_per_tile == 0:
            k0 = POOL_WIDTH + (g + 1 - groups_per_tile) * GROUP_DIM
            acc = acc + _dot(ycat_ref[:, k0:k0 + MXU_DIM], wout_ref[k0:k0 + MXU_DIM, :])
    o_ref[0] = acc


def _even_mixer(h, g, w_in, pool_w, pool_scale, sgu_norm, sgu_w, sgu_b_t, w_out):
    b, s, d = h.shape
    ts = SEQ_TILE
    return pl.pallas_call(
        _even_kernel,
        grid=(b, s // ts),
        in_specs=[
            pl.BlockSpec((1, ts, d), lambda i, j: (i, j, 0)),
            _const_spec((1, d)),
            _const_spec((d, EVEN_IN)),
            _const_spec(pool_w.shape),
            _const_spec((1, POOL_WIDTH)),
            _const_spec((1, SGU_WIDTH)),
            _const_spec(sgu_w.shape),
            _const_spec(sgu_b_t.shape),
            _const_spec((POOL_WIDTH + SGU_WIDTH, d)),
        ],
        out_specs=pl.BlockSpec((1, ts, d), lambda i, j: (i, j, 0)),
        out_shape=jax.ShapeDtypeStruct(h.shape, F32),
        scratch_shapes=[pltpu.VMEM((POOL_HALO + ts, POOL_WIDTH), F32)] * 3 + [
            pltpu.VMEM((ts, SGU_WIDTH), F32),
            pltpu.VMEM((ts, SGU_WIDTH), F32),
            pltpu.VMEM((ts, POOL_WIDTH + SGU_WIDTH), BF16),
        ],
        compiler_params=_params("arbitrary", "arbitrary"),
        name="even_mixer",
    )(h, g, w_in, pool_w, pool_scale, sgu_norm, sgu_w, sgu_b_t, w_out)


def _kv_kernel(m_ref, g_ref, w_ref, o_ref):
    mn = _rms(m_ref[0], g_ref[0]).astype(BF16)
    o_ref[0, 0] = _dot(mn, w_ref[0]).astype(BF16)


def _memory_kv(mem, mem_norm, wkv):
    b, m, d = mem.shape
    depth = wkv.shape[0]
    return pl.pallas_call(
        _kv_kernel,
        grid=(depth, b),
        in_specs=[
            pl.BlockSpec((1, m, d), lambda l, i: (i, 0, 0)),
            pl.BlockSpec((1, 1, d), lambda l, i: (l, 0, 0)),
            pl.BlockSpec((1, d, 2 * d), lambda l, i: (l, 0, 0)),
        ],
        out_specs=pl.BlockSpec((1, 1, m, 2 * d), lambda l, i: (l, i, 0, 0)),
        out_shape=jax.ShapeDtypeStruct((depth, b, m, 2 * d), BF16),
        compiler_params=_params("arbitrary", "arbitrary"),
        name="memory_kv",
    )(mem, mem_norm.reshape(depth, 1, d), wkv)


def _xattn_body(x_ref, kv_ref, g_ref, wq_ref, wo_ref, o_ref, q_ref, ocat_ref, s_ref, p_ref):
    ts = x_ref.shape[0]
    top, bot = slice(0, ts // 2), slice(ts // 2, ts)
    q_scale = (XATTN_HEAD_DIM ** -0.5) * math.log2(math.e)

    def cols(h):
        return slice(h * XATTN_HEAD_DIM, (h + 1) * XATTN_HEAD_DIM)

    def q_proj(rows):
        hn = _rms(x_ref[rows, :], g_ref[...]).astype(BF16)
        q_ref[rows, :] = (_dot(hn, wq_ref[...]) * q_scale).astype(BF16)

    def scores(rows):
        for h in range(XATTN_HEADS):
            s_ref[h] = _dot_nt(q_ref[rows, cols(h)], kv_ref[0, 0, :, cols(h)])

    def softmax():
        for h in range(XATTN_HEADS):
            s = s_ref[h]
            p = jnp.exp2(s - jnp.max(s, axis=-1, keepdims=True))
            p_ref[h] = (p * (1.0 / jnp.sum(p, axis=-1, keepdims=True))).astype(BF16)

    def values(rows):
        for h in range(XATTN_HEADS):
            vh = kv_ref[0, 0, :, D_MODEL + h * XATTN_HEAD_DIM:D_MODEL + (h + 1) * XATTN_HEAD_DIM]
            ocat_ref[rows, cols(h)] = _dot(p_ref[h], vh).astype(BF16)

    def out_proj(rows):
        o_ref[0, rows, :] = x_ref[rows, :] + _dot(ocat_ref[rows, :], wo_ref[...])

    q_proj(top)
    scores(top)
    q_proj(bot)
    softmax()
    values(top)
    scores(bot)
    out_proj(top)
    softmax()
    values(bot)
    out_proj(bot)


def _xattn_kernel(h_ref, kv_ref, g_ref, wq_ref, wo_ref, o_ref, q_ref, ocat_ref, s_ref, p_ref):
    _xattn_body(h_ref.at[0], kv_ref, g_ref, wq_ref, wo_ref, o_ref, q_ref, ocat_ref, s_ref, p_ref)


def _mix_xattn_kernel(h_ref, yc_ref, yd_ref, wmix_ref, kv_ref, g_ref, wq_ref, wo_ref, o_ref,
                      q_ref, ocat_ref, s_ref, p_ref, x_ref):
    ts = h_ref.shape[1]
    for r in range(0, ts, ts // 2):
        rows = slice(r, r + ts // 2)
        x_ref[rows, :] = (h_ref[0, rows, :] + _dot(yc_ref[0, rows, :], wmix_ref[:DIL_WIDTH, :])
                          + _dot(yd_ref[0, rows, :], wmix_ref[DIL_WIDTH:, :]))
    _xattn_body(x_ref, kv_ref, g_ref, wq_ref, wo_ref, o_ref, q_ref, ocat_ref, s_ref, p_ref)


def _xattn(h, kv, layer, g, wq, wo, mix=None):
    b, s, d = h.shape
    ts = SEQ_TILE
    tile = pl.BlockSpec((1, ts, d), lambda i, j: (i, j, 0))
    kv_spec = pl.BlockSpec((1, 1, N_MEM, 2 * d), lambda i, j: (layer, i, 0, 0))
    tail_specs = [kv_spec, _const_spec((1, d)), _const_spec((d, d)), _const_spec((d, d))]
    if mix is None:
        kern, in_specs, args = _xattn_kernel, [tile] + tail_specs, (h, kv, g, wq, wo)
    else:
        yc, yd, w_mix = mix
        half = pl.BlockSpec((1, ts, DIL_WIDTH), lambda i, j: (i, j, 0))
        kern = _mix_xattn_kernel
        in_specs = [tile, half, half, _const_spec((d, d))] + tail_specs
        args = (h, yc, yd, w_mix, kv, g, wq, wo)
    return pl.pallas_call(
        kern,
        grid=(b, s // ts),
        in_specs=in_specs,
        out_specs=tile,
        out_shape=jax.ShapeDtypeStruct(h.shape, F32),
        scratch_shapes=[
            pltpu.VMEM((ts, d), BF16),
            pltpu.VMEM((ts, d), BF16),
            pltpu.VMEM((XATTN_HEADS, ts // 2, N_MEM), F32),
            pltpu.VMEM((XATTN_HEADS, ts // 2, N_MEM), BF16),
        ] + ([] if mix is None else [pltpu.VMEM((ts, d), F32)]),
        compiler_params=_params("arbitrary", "arbitrary"),
        name="xattn" if mix is None else "mix_xattn",
    )(*args)


def _ffn_kernel(x_ref, g_ref, wgu_ref, wd_ref, fg_ref, o_ref, *, final_norm):
    x = x_ref[...]
    tm = x.shape[0]
    hn_rows, gate_rows = [], []
    for r in range(0, tm, tm // 2):
        hn_rows.append(_rms(x[r:r + tm // 2], g_ref[...]).astype(BF16))
        gate_rows.append(_dot(hn_rows[-1], wgu_ref[:, FF_CHUNKS[0][0]:FF_CHUNKS[0][1]]))
    hn = jnp.concatenate(hn_rows, axis=0)
    acc = x
    for c, (lo, hi) in enumerate(FF_CHUNKS):
        gate = jnp.concatenate(gate_rows, axis=0) if c == 0 else _dot(hn, wgu_ref[:, lo:hi])
        up = _dot(hn, wgu_ref[:, D_FF + lo:D_FF + hi])
        act = (_silu(gate) * up).astype(BF16)
        acc = acc + _dot(act, wd_ref[lo:hi, :])
    if final_norm:
        acc = _rms(acc, fg_ref[...])
    o_ref[...] = acc


def _ffn(h, g, wgu, wd, fg, final_norm):
    b, s, d = h.shape
    t = b * s
    tm = SEQ_TILE
    tile = pl.BlockSpec((tm, d), lambda i: (i, 0))
    out = pl.pallas_call(
        functools.partial(_ffn_kernel, final_norm=final_norm),
        grid=(t // tm,),
        in_specs=[tile, _const_spec((1, d)), _const_spec((d, 2 * D_FF)), _const_spec((D_FF, d)),
                  _const_spec((1, d))],
        out_specs=tile,
        out_shape=jax.ShapeDtypeStruct((t, d), F32),
        compiler_params=_params("arbitrary"),
        name="ffn_final" if final_norm else "ffn",
    )(h.reshape(t, d), g, wgu, wd, fg)
    return out.reshape(b, s, d)


def _odd_proj_kernel(x_ref, g_ref, w_ref, wkt_ref, qc_ref, qsa_ref, qsb_ref, dc_ref, dsa_ref, dsb_ref,
                     rc_ref, rs_ref, kct_ref, kst_ref,
                     cq_ref, ck_ref, cv_ref, rq_ref, rkt_ref, rv_ref, rg_ref):
    ts = x_ref.shape[1]
    hn_rows, first_rows = [], []
    for r in range(0, ts, ts // 2):
        hn_rows.append(_rms(x_ref[0, r:r + ts // 2, :], g_ref[...]).astype(BF16))
        first_rows.append(_dot(hn_rows[-1], w_ref[:, :DIL_WIDTH]))
    hn = jnp.concatenate(hn_rows, axis=0)

    def section(i):
        if i == 0:
            return jnp.concatenate(first_rows, axis=0)
        return _dot(hn, w_ref[:, i * DIL_WIDTH:(i + 1) * DIL_WIDTH])

    def lane_blocks(z, fn):
        return jnp.concatenate([fn(z[:, j * LANES:(j + 1) * LANES]) for j in range(z.shape[1] // LANES)], axis=1)

    half = ROPE_DIM // 2

    def dil_rot(c_ref, sa_ref, sb_ref):
        c, sa, sb = c_ref[...], sa_ref[...], sb_ref[...]
        return lambda zb: zb * c + pltpu.roll(zb, half, 1) * sa + pltpu.roll(zb, LANES - half, 1) * sb

    cq_ref[0] = lane_blocks(section(0), dil_rot(qc_ref, qsa_ref, qsb_ref))
    ck_ref[0] = lane_blocks(section(1), dil_rot(dc_ref, dsa_ref, dsb_ref))
    cv_ref[0] = section(2)

    rc, rs = rc_ref[...], rs_ref[...]

    def ret_rot(zb):
        return zb * rc + pltpu.roll(zb, RET_HEAD_DIM // 2, 1) * rs

    rq_ref[0] = lane_blocks(section(3), ret_rot).astype(BF16)
    rv_ref[0] = section(5).astype(BF16)
    rg_ref[0] = section(6)

    kt = _dot_nt(wkt_ref[...], hn)
    kct, kst = kct_ref[...], kst_ref[...]
    hd = RET_HEAD_DIM
    for h in range(RET_HEADS):
        zb = kt[h * hd:(h + 1) * hd, :]
        swapped = jnp.concatenate([zb[hd // 2:], zb[:hd // 2]], axis=0)
        rkt_ref[0, h * hd:(h + 1) * hd, :] = (zb * kct + swapped * kst).astype(BF16)


def _rotary_tables(s):
    pos = jnp.arange(s, dtype=jnp.int32).astype(F32)
    lane = jnp.arange(LANES)
    half = ROPE_DIM // 2
    inv = 1.0 / jnp.power(jnp.float32(ROPE_THETA), jnp.arange(half, dtype=F32) / half)
    ang = pos[:, None] * inv[None, :]
    cos, sin = jnp.cos(ang), jnp.sin(ang)
    hl = lane % DIL_HEAD_DIM
    cos_l, sin_l = cos[:, hl % half], sin[:, hl % half]
    dc = jnp.where(hl[None] < ROPE_DIM, cos_l, 1.0)
    dsa = jnp.where((hl[None] >= half) & (hl[None] < ROPE_DIM), sin_l, 0.0)
    dsb = jnp.where(hl[None] < half, -sin_l, 0.0)
    rhalf = RET_HEAD_DIM // 2
    rinv = 1.0 / jnp.power(jnp.float32(RET_THETA), jnp.arange(rhalf, dtype=F32) / rhalf)
    rang = pos[:, None] * rinv[None, :]
    rcos, rsin = jnp.cos(rang), jnp.sin(rang)
    rc = jnp.concatenate([rcos, rcos], axis=1)
    rs = jnp.concatenate([-rsin, rsin], axis=1)
    qs = (DIL_HEAD_DIM ** -0.5) * math.log2(math.e)
    ks = RET_HEAD_DIM ** -0.5
    return (dc * qs, dsa * qs, dsb * qs, dc, dsa, dsb, rc, rs), ((rc * ks).T, (rs * ks).T)


def _odd_proj(h, g, w_in):
    b, s, d = h.shape
    ts = SEQ_TILE
    tables, tables_t = _rotary_tables(s)
    wkt = w_in[:, 3 * DIL_WIDTH + RET_WIDTH:3 * DIL_WIDTH + 2 * RET_WIDTH].T
    tile = pl.BlockSpec((1, ts, d), lambda j, i: (i, j, 0))
    tab = pl.BlockSpec((ts, LANES), lambda j, i: (j, 0))
    tab_t = pl.BlockSpec((LANES, ts), lambda j, i: (0, j))
    out = pl.BlockSpec((1, ts, DIL_WIDTH), lambda j, i: (i, j, 0))
    out_t = pl.BlockSpec((1, RET_WIDTH, ts), lambda j, i: (i, 0, j))
    f32o = jax.ShapeDtypeStruct((b, s, DIL_WIDTH), F32)
    bf16o = jax.ShapeDtypeStruct((b, s, RET_WIDTH), BF16)
    return pl.pallas_call(
        _odd_proj_kernel,
        grid=(s // ts, b),
        in_specs=[tile, _const_spec((1, d)), _const_spec((d, ODD_IN)), _const_spec((RET_WIDTH, d))]
        + [tab] * len(tables) + [tab_t] * len(tables_t),
        out_specs=[out, out, out, out, out_t, out, out],
        out_shape=[f32o, f32o, f32o, bf16o, jax.ShapeDtypeStruct((b, RET_WIDTH, s), BF16), bf16o, f32o],
        compiler_params=_params("arbitrary", "arbitrary"),
        name="odd_proj",
    )(h, g, w_in, wkt, *tables, *tables_t)


def _dilated_kernel(q_ref, k_ref, v_ref, o_ref, acc_ref, m_ref, l_ref, sel_ref, bias_ref, s_ref, p_ref, mcur_ref):
    s_len = q_ref.shape[1]
    blk = DIL_BLOCK
    lane = lax.broadcasted_iota(jnp.int32, (blk, LANES), 1)
    first_head = lane < DIL_HEAD_DIM

    qi = lax.broadcasted_iota(jnp.int32, (2 * blk, blk), 1)
    kr = lax.broadcasted_iota(jnp.int32, (2 * blk, blk), 0)
    sel_ref[...] = jnp.where(kr % blk == qi, 1.0, 0.0).astype(BF16)
    bias_ref[0] = jnp.where(jnp.abs(qi + blk - kr - blk // 2) <= blk // 2, 0.0, -MASK_BIG).astype(BF16)
    bias_ref[1] = jnp.where(kr <= qi, 0.0, -MASK_BIG).astype(BF16)

    blocks = []
    for branch, (window, dil) in reversed(list(enumerate(DIL_PAIRS))):
        assert window // dil == blk
        span = dil * blk
        for r in range(dil):
            blocks += [(branch, dil, r + n * span, n > 0) for n in range(s_len // span)]
    assert DIL_PAIRS[0][1] == 1

    def rows(t, prev=False):
        _, dil, start, _ = blocks[t]
        start -= dil * blk if prev else 0
        return pl.ds(start, blk) if dil == 1 else pl.ds(start, blk, stride=dil)

    def load2(ref, t):
        x = ref[0, rows(t), :].astype(BF16)
        return jnp.concatenate([ref[0, rows(t, True), :].astype(BF16), x], axis=0) if blocks[t][3] else x

    def n_keys(t):
        return 2 * blk if blocks[t][3] else blk

    def scores(t):
        q = q_ref[0, rows(t), :]
        q2 = jnp.concatenate([jnp.where(first_head, q, 0.0), jnp.where(first_head, 0.0, q)], axis=0).astype(BF16)
        bias = bias_ref[0] if blocks[t][3] else bias_ref[1, :blk, :]
        s_ref[t % DIL_SLOTS, :, :n_keys(t)] = _dot_nt(jnp.concatenate([q2, sel_ref[...]], axis=1),
                                                      jnp.concatenate([load2(k_ref, t), bias], axis=1))

    def softmax(t):
        s = s_ref[t % DIL_SLOTS, :, :n_keys(t)]
        m = jnp.max(s, axis=-1, keepdims=True)
        p_ref[t % DIL_SLOTS, :, :n_keys(t)] = jnp.exp2(s - m).astype(BF16)
        mm = jnp.where(first_head, m[:blk], m[blk:])
        if blocks[t][0] == 0:
            mcur_ref[t % DIL_SLOTS] = mm
        else:
            m_ref[blocks[t][0] - 1, rows(t), :] = mm

    def values(t):
        vext = jnp.concatenate([load2(v_ref, t), jnp.ones((n_keys(t), LANES), BF16)], axis=1)
        o2 = _dot(p_ref[t % DIL_SLOTS, :, :n_keys(t)], vext)
        acc = jnp.where(first_head, o2[:blk, :LANES], o2[blk:, :LANES])
        den = jnp.where(first_head, o2[:blk, LANES:], o2[blk:, LANES:])
        branch = blocks[t][0]
        if branch > 0:
            acc_ref[branch - 1, rows(t), :] = acc
            l_ref[branch - 1, rows(t), :] = den
            return
        m_own = mcur_ref[t % DIL_SLOTS]
        m_oth = [m_ref[j, rows(t), :] for j in range(len(DIL_PAIRS) - 1)]
        m_all = functools.reduce(jnp.maximum, m_oth, m_own)
        w = jnp.exp2(m_own - m_all)
        num, den = w * acc, w * den
        for j, mj in enumerate(m_oth):
            w = jnp.exp2(mj - m_all)
            num = num + w * acc_ref[j, rows(t), :]
            den = den + w * l_ref[j, rows(t), :]
        o_ref[0, rows(t), :] = (num / den).astype(BF16)

    for t in range(len(blocks) + 2):
        if t < len(blocks):
            scores(t)
        if 0 <= t - 1 < len(blocks):
            softmax(t - 1)
        if 0 <= t - 2 < len(blocks):
            values(t - 2)


def _dilated_attention(cq, ck, cv):
    b, s, _ = cq.shape
    spec = pl.BlockSpec((1, s, LANES), lambda i, p: (i, 0, p))
    nbr = len(DIL_PAIRS)
    return pl.pallas_call(
        _dilated_kernel,
        grid=(b, DIL_WIDTH // LANES),
        in_specs=[spec] * 3,
        out_specs=spec,
        out_shape=jax.ShapeDtypeStruct((b, s, DIL_WIDTH), BF16),
        scratch_shapes=[pltpu.VMEM((nbr - 1, s, LANES), F32)] * 3 + [
            pltpu.VMEM((2 * DIL_BLOCK, LANES), BF16),
            pltpu.VMEM((2, 2 * DIL_BLOCK, LANES), BF16),
            pltpu.VMEM((DIL_SLOTS, 2 * DIL_BLOCK, 2 * DIL_BLOCK), F32),
            pltpu.VMEM((DIL_SLOTS, 2 * DIL_BLOCK, 2 * DIL_BLOCK), BF16),
            pltpu.VMEM((DIL_SLOTS, DIL_BLOCK, LANES), F32),
        ],
        compiler_params=_params("arbitrary", "arbitrary"),
        name="dilated_attention",
    )(cq, ck, cv)


def _retention_kernel(q_ref, kt_ref, v_ref, g_ref, dec_ref, kdec_ref, qdec_ref, o_ref,
                      y_ref, kv_ref, st_ref, s_ref, in_ref):
    s_len = q_ref.shape[1]
    c_len = RET_CHUNK
    n_chunks = s_len // c_len
    decay = dec_ref[0]
    kdec = kdec_ref[0]
    qdec = qdec_ref[0]
    g_chunk = qdec[c_len - 1:c_len, :]

    def chunk(c):
        return slice(c * c_len, (c + 1) * c_len)

    def decay_scores(c):
        in_ref[c] = (s_ref[c] * decay).astype(BF16)

    for c in range(n_chunks + 1):
        if c < n_chunks:
            ktc = kt_ref[0, :, chunk(c)]
            s_ref[c] = _dot(q_ref[0, chunk(c), :], ktc)
            kv_ref[c] = _dot((ktc.astype(F32) * kdec).astype(BF16), v_ref[0, chunk(c), :])
        if c >= 1:
            decay_scores(c - 1)
    state = jnp.zeros((RET_HEAD_DIM, RET_HEAD_DIM), F32)
    for c in range(n_chunks):
        st_ref[c] = state.astype(BF16)
        state = state * g_chunk + kv_ref[c]

    def outputs(c):
        y_ref[chunk(c), :] = (_dot(in_ref[c], v_ref[0, chunk(c), :])
                              + _dot(q_ref[0, chunk(c), :], st_ref[c]) * qdec)

    def normalise(c):
        y = y_ref[chunk(c), :]
        mu = jnp.mean(y, axis=-1, keepdims=True)
        yc = y - mu
        var = jnp.mean(yc * yc, axis=-1, keepdims=True)
        o_ref[0, chunk(c), :] = (_silu(g_ref[0, chunk(c), :]) * (yc * lax.rsqrt(var + EPS))).astype(BF16)

    for c in range(n_chunks + 1):
        if c < n_chunks:
            outputs(c)
        if c >= 1:
            normalise(c - 1)


def _retention_tables():
    c_len = RET_CHUNK
    gamma = 1.0 - jnp.power(2.0, -5.0 - jnp.arange(RET_HEADS, dtype=F32))
    log_g = jnp.log(gamma)
    idx = jnp.arange(c_len, dtype=F32)
    rel = idx[:, None] - idx[None, :]
    decay = jnp.where(rel[None] >= 0, jnp.exp(jnp.maximum(rel, 0.0)[None] * log_g[:, None, None]), 0.0)
    k_decay = jnp.exp((c_len - 1 - idx)[None, :] * log_g[:, None])
    q_decay = jnp.exp((idx + 1)[None, :] * log_g[:, None])
    return (decay, jnp.broadcast_to(k_decay[:, None, :], (RET_HEADS, RET_HEAD_DIM, c_len)),
            jnp.broadcast_to(q_decay[:, :, None], (RET_HEADS, c_len, RET_HEAD_DIM)))


def _retention(rq, rkt, rv, rg):
    b, s, _ = rq.shape
    hd = RET_HEAD_DIM
    spec = pl.BlockSpec((1, s, hd), lambda i, h: (i, 0, h))
    spec_t = pl.BlockSpec((1, hd, s), lambda i, h: (i, h, 0))
    tables = _retention_tables()
    return pl.pallas_call(
        _retention_kernel,
        grid=(b, RET_HEADS),
        in_specs=[spec, spec_t, spec, spec]
        + [pl.BlockSpec((1,) + t.shape[1:], lambda i, h: (h, 0, 0)) for t in tables],
        out_specs=spec,
        out_shape=jax.ShapeDtypeStruct((b, s, RET_WIDTH), BF16),
        scratch_shapes=[
            pltpu.VMEM((s, hd), F32),
            pltpu.VMEM((s // RET_CHUNK, hd, hd), F32),
            pltpu.VMEM((s // RET_CHUNK, hd, hd), BF16),
            pltpu.VMEM((s // RET_CHUNK, RET_CHUNK, RET_CHUNK), F32),
            pltpu.VMEM((s // RET_CHUNK, RET_CHUNK, RET_CHUNK), BF16),
        ],
        compiler_params=_params("arbitrary", "arbitrary"),
        name="retention",
    )(rq, rkt, rv, rg, *tables)


def kernel(x, mem, even_mix_norm, even_w_in, pool_w, pool_scale, sgu_norm, sgu_w, sgu_b, even_w_out,
           odd_mix_norm, odd_w_in, odd_w_out, xattn_norm, mem_norm, xattn_wq, xattn_wkv, xattn_wo,
           ffn_norm, ffn_w_gate_up, ffn_w_down, final_norm):
    depth = xattn_wq.shape[0]
    assert x.shape[1] % (max(d for _, d in DIL_PAIRS) * DIL_BLOCK) == 0 and x.shape[2] == D_MODEL

    def row(v):
        return v.reshape(1, -1)

    kv = _memory_kv(mem, mem_norm, xattn_wkv.astype(BF16))
    h = x
    for layer in range(depth):
        i = layer // 2
        if layer % 2 == 0:
            h = _even_mixer(h, row(even_mix_norm[i]), even_w_in[i].astype(BF16), pool_w[i].astype(BF16),
                            row(pool_scale[i]), row(sgu_norm[i]), sgu_w[i], sgu_b[i].T,
                            even_w_out[i].astype(BF16))
            mix = None
        else:
            cq, ck, cv, rq, rk, rv, rg = _odd_proj(h, row(odd_mix_norm[i]), odd_w_in[i].astype(BF16))
            yc = _dilated_attention(cq, ck, cv)
            yd = _retention(rq, rk, rv, rg)
            mix = (yc, yd, odd_w_out[i].astype(BF16))
        h = _xattn(h, kv, layer, row(xattn_norm[layer]), xattn_wq[layer].astype(BF16),
                   xattn_wo[layer].astype(BF16), mix)
        h = _ffn(h, row(ffn_norm[layer]), ffn_w_gate_up[layer].astype(BF16), ffn_w_down[layer].astype(BF16),
                 row(final_norm), final_norm=layer == depth - 1)
    return h
```

```python
import functools
import math

import jax
import jax.numpy as jnp
from jax import lax
from jax.experimental import pallas as pl
from jax.experimental.pallas import tpu as pltpu

F32 = jnp.float32
BF16 = jnp.bfloat16

EPS = 1e-6
D_MODEL = 1024
N_MEM = 256

POOL_WINDOWS = (2, 4, 8, 16)
POOL_HALO = 8 * len(POOL_WINDOWS)
assert POOL_WINDOWS == tuple(2 ** (g + 1) for g in range(len(POOL_WINDOWS)))
POOL_WIDTH = 512
GROUP_DIM = 128
SGU_WIDTH = 512
SGU_CHUNK = 128
EVEN_IN = POOL_WIDTH + 2 * SGU_WIDTH

DIL_HEADS = 8
DIL_HEAD_DIM = 64
DIL_WIDTH = 512
DIL_PAIRS = ((128, 1), (512, 4), (2048, 16))
DIL_BLOCK = 128
DIL_SLOTS = 3
MASK_BIG = 2.0 ** 126
ROPE_THETA = 500000.0
ROPE_DIM = 16
RET_HEADS = 4
RET_HEAD_DIM = 128
RET_WIDTH = 512
RET_CHUNK = 256
RET_THETA = 10000.0
ODD_IN = 3 * DIL_WIDTH + 4 * RET_WIDTH

XATTN_HEADS = 4
XATTN_HEAD_DIM = 256
D_FF = 2816

LANES = 128
MXU_DIM = 256
FF_STEP = 6 * MXU_DIM
FF_CHUNKS = tuple((lo, min(lo + FF_STEP, D_FF)) for lo in range(0, D_FF, FF_STEP))
SEQ_TILE = 512
WIDE_TILE = 1024
VMEM_LIMIT = 56 * 1024 * 1024


def _params(*sem):
    return pltpu.CompilerParams(dimension_semantics=sem, vmem_limit_bytes=VMEM_LIMIT)


def _const_spec(shape):
    zeros = (0,) * len(shape)
    return pl.BlockSpec(shape, lambda *_: zeros, pipeline_mode=pl.Buffered(1))


def _rms(x, g):
    ms = jnp.mean(x * x, axis=-1, keepdims=True)
    return x * lax.rsqrt(ms + EPS) * g


def _gelu_tanh(x):
    return 0.5 * x * (1.0 + jnp.tanh(math.sqrt(2.0 / math.pi) * (x + 0.044715 * (x * x * x))))


def _silu(x):
    return x * (1.0 / (1.0 + jnp.exp(-x)))


def _dot(a, b):
    return jnp.dot(a, b, preferred_element_type=F32)


def _dot_nt(a, b):
    return lax.dot_general(a, b, (((1,), (1,)), ((), ())), preferred_element_type=F32)


def _even_kernel(x_ref, g_ref, win_ref, pw_ref, ps_ref, sn_ref, sw_ref, sb_ref, wout_ref,
                 o_ref, ext_ref, lva_ref, lvb_ref, u_ref, v_ref, ycat_ref):
    si = pl.program_id(1)
    ts = x_ref.shape[1]
    halo, end = POOL_HALO, POOL_HALO + ts
    half = MXU_DIM
    x = x_ref[0]

    @pl.when(si == 0)
    def _():
        ext_ref[0:halo, :] = jnp.zeros((halo, POOL_WIDTH), F32)

    hn_rows = []
    for r in range(0, ts, ts // 2):
        hn_rows.append(_rms(x[r:r + ts // 2], g_ref[...]).astype(BF16))
        ext_ref[halo + r:halo + r + ts // 2, :] = _dot(hn_rows[-1], win_ref[:, :POOL_WIDTH])
    hn = jnp.concatenate(hn_rows, axis=0)

    def in_proj(ref, c0, c1):
        ref[:, c0 - c1:c0 - c1 + half] = _dot(hn, win_ref[:, c0:c0 + half])

    def gelu_inplace(ref, c):
        ref[:, c:c + half] = _gelu_tanh(ref[:, c:c + half])

    bufs = (lva_ref, lvb_ref)

    def level(k):
        src = ext_ref if k == 1 else bufs[k % 2]
        dst, lo, back, c0 = bufs[(k - 1) % 2], 8 * k, 2 ** (k - 1), (k - 1) * GROUP_DIM
        dst[lo:end, c0:] = src[lo:end, c0:] + src[lo - back:end - back, c0:]

    u0, v0 = POOL_WIDTH, POOL_WIDTH + SGU_WIDTH
    in_proj(u_ref, u0, u0)
    level(1)
    level(2)
    in_proj(u_ref, u0 + half, u0)
    level(3)
    level(4)
    assert len(POOL_WINDOWS) == 4

    in_proj(v_ref, v0, v0)
    pos = si * ts + lax.broadcasted_iota(jnp.int32, (ts, 1), 0)
    for g, win in enumerate(POOL_WINDOWS):
        cols = slice(g * GROUP_DIM, (g + 1) * GROUP_DIM)
        wsum = bufs[g % 2][halo:end, cols]
        cnt = jnp.minimum(pos + 1, win).astype(F32)
        d = (wsum / cnt - ext_ref[halo:end, cols]).astype(BF16)
        ya = _dot(d, pw_ref[g]) * ps_ref[:, cols]
        ycat_ref[:, cols] = ya.astype(BF16)
    ext_ref[0:halo, :] = ext_ref[ts:end, :]

    in_proj(v_ref, v0 + half, v0)
    gelu_inplace(u_ref, 0)
    gelu_inplace(u_ref, half)

    acc = x + _dot(ycat_ref[:, :POOL_WIDTH], wout_ref[:POOL_WIDTH, :])
    gelu_inplace(v_ref, 0)
    gelu_inplace(v_ref, half)
    vn = _rms(v_ref[...], sn_ref[...]).astype(BF16)

    nc = ts // SGU_CHUNK
    row = lax.broadcasted_iota(jnp.int32, (SGU_CHUNK, SGU_CHUNK), 0)
    col = lax.broadcasted_iota(jnp.int32, (SGU_CHUNK, SGU_CHUNK), 1)
    groups_per_tile = MXU_DIM // GROUP_DIM
    for g in range(SGU_WIDTH // GROUP_DIM):
        cols = slice(g * GROUP_DIM, (g + 1) * GROUP_DIM)
        ws = jnp.where(row >= col, sw_ref[g], 0.0).astype(BF16)
        rhs = jnp.concatenate([vn[c * SGU_CHUNK:(c + 1) * SGU_CHUNK, cols] for c in range(nc)], axis=1)
        mixed = _dot(ws, rhs) + sb_ref[:, g:g + 1]
        for c in range(nc):
            rows = slice(c * SGU_CHUNK, (c + 1) * SGU_CHUNK)
            yb = u_ref[rows, cols] * mixed[:, c * SGU_CHUNK:(c + 1) * SGU_CHUNK]
            ycat_ref[rows, POOL_WIDTH + g * GROUP_DIM:POOL_WIDTH + (g + 1) * GROUP_DIM] = yb.astype(BF16)
        if (g + 1) % groups_per_tile == 0:
            k0 = POOL_WIDTH + (g + 1 - groups_per_tile) * GROUP_DIM
            acc = acc + _dot(ycat_ref[:, k0:k0 + MXU_DIM], wout_ref[k0:k0 + MXU_DIM, :])
    o_ref[0] = acc


def _even_mixer(h, g, w_in, pool_w, pool_scale, sgu_norm, sgu_w, sgu_b_t, w_out):
    b, s, d = h.shape
    ts = WIDE_TILE
    return pl.pallas_call(
        _even_kernel,
        grid=(b, s // ts),
        in_specs=[
            pl.BlockSpec((1, ts, d), lambda i, j: (i, j, 0)),
            _const_spec((1, d)),
            _const_spec((d, EVEN_IN)),
            _const_spec(pool_w.shape),
            _const_spec((1, POOL_WIDTH)),
            _const_spec((1, SGU_WIDTH)),
            _const_spec(sgu_w.shape),
            _const_spec(sgu_b_t.shape),
            _const_spec((POOL_WIDTH + SGU_WIDTH, d)),
        ],
        out_specs=pl.BlockSpec((1, ts, d), lambda i, j: (i, j, 0)),
        out_shape=jax.ShapeDtypeStruct(h.shape, F32),
        scratch_shapes=[pltpu.VMEM((POOL_HALO + ts, POOL_WIDTH), F32)] * 3 + [
            pltpu.VMEM((ts, SGU_WIDTH), F32),
            pltpu.VMEM((ts, SGU_WIDTH), F32),
            pltpu.VMEM((ts, POOL_WIDTH + SGU_WIDTH), BF16),
        ],
        compiler_params=_params("arbitrary", "arbitrary"),
        name="even_mixer",
    )(h, g, w_in, pool_w, pool_scale, sgu_norm, sgu_w, sgu_b_t, w_out)


def _kv_kernel(m_ref, g_ref, w_ref, o_ref):
    mn = _rms(m_ref[...], g_ref[0]).astype(BF16)
    o_ref[0] = _dot(mn, w_ref[0]).astype(BF16)


def _memory_kv(mem, mem_norm, wkv):
    b, m, d = mem.shape
    depth = wkv.shape[0]
    rows = b * m
    tm = math.gcd(rows, 2 * SEQ_TILE)
    out = pl.pallas_call(
        _kv_kernel,
        grid=(depth, rows // tm),
        in_specs=[
            pl.BlockSpec((tm, d), lambda l, i: (i, 0)),
            pl.BlockSpec((1, 1, d), lambda l, i: (l, 0, 0)),
            pl.BlockSpec((1, d, 2 * d), lambda l, i: (l, 0, 0)),
        ],
        out_specs=pl.BlockSpec((1, tm, 2 * d), lambda l, i: (l, i, 0)),
        out_shape=jax.ShapeDtypeStruct((depth, rows, 2 * d), BF16),
        compiler_params=_params("arbitrary", "arbitrary"),
        name="memory_kv",
    )(mem.reshape(rows, d), mem_norm.reshape(depth, 1, d), wkv)
    return out.reshape(depth, b, m, 2 * d)


def _xattn_body(x_ref, kv_ref, g_ref, wq_ref, wo_ref, o_ref, q_ref, ocat_ref, s_ref, p_ref):
    ts = x_ref.shape[0]
    top, bot = slice(0, ts // 2), slice(ts // 2, ts)
    q_scale = (XATTN_HEAD_DIM ** -0.5) * math.log2(math.e)

    def cols(h):
        return slice(h * XATTN_HEAD_DIM, (h + 1) * XATTN_HEAD_DIM)

    def q_proj(rows):
        hn = _rms(x_ref[rows, :], g_ref[...]).astype(BF16)
        q_ref[rows, :] = (_dot(hn, wq_ref[...]) * q_scale).astype(BF16)

    def scores(rows):
        for h in range(XATTN_HEADS):
            s_ref[h] = _dot_nt(q_ref[rows, cols(h)], kv_ref[0, 0, :, cols(h)])

    def softmax():
        for h in range(XATTN_HEADS):
            s = s_ref[h]
            p = jnp.exp2(s - jnp.max(s, axis=-1, keepdims=True))
            p_ref[h] = (p * (1.0 / jnp.sum(p, axis=-1, keepdims=True))).astype(BF16)

    def values(rows):
        for h in range(XATTN_HEADS):
            vh = kv_ref[0, 0, :, D_MODEL + h * XATTN_HEAD_DIM:D_MODEL + (h + 1) * XATTN_HEAD_DIM]
            ocat_ref[rows, cols(h)] = _dot(p_ref[h], vh).astype(BF16)

    def out_proj(rows):
        o_ref[0, rows, :] = x_ref[rows, :] + _dot(ocat_ref[rows, :], wo_ref[...])

    q_proj(top)
    scores(top)
    q_proj(bot)
    softmax()
    values(top)
    scores(bot)
    out_proj(top)
    softmax()
    values(bot)
    out_proj(bot)


def _xattn_kernel(h_ref, kv_ref, g_ref, wq_ref, wo_ref, o_ref, q_ref, ocat_ref, s_ref, p_ref):
    _xattn_body(h_ref.at[0], kv_ref, g_ref, wq_ref, wo_ref, o_ref, q_ref, ocat_ref, s_ref, p_ref)


def _mix_xattn_kernel(h_ref, yc_ref, yd_ref, wmix_ref, kv_ref, g_ref, wq_ref, wo_ref, o_ref,
                      q_ref, ocat_ref, s_ref, p_ref, x_ref):
    ts = h_ref.shape[1]
    for r in range(0, ts, ts // 2):
        rows = slice(r, r + ts // 2)
        x_ref[rows, :] = (h_ref[0, rows, :] + _dot(yc_ref[0, rows, :], wmix_ref[:DIL_WIDTH, :])
                          + _dot(yd_ref[0, rows, :], wmix_ref[DIL_WIDTH:, :]))
    _xattn_body(x_ref, kv_ref, g_ref, wq_ref, wo_ref, o_ref, q_ref, ocat_ref, s_ref, p_ref)


def _xattn(h, kv, layer, g, wq, wo, mix=None):
    b, s, d = h.shape
    ts = WIDE_TILE
    tile = pl.BlockSpec((1, ts, d), lambda i, j: (i, j, 0))
    kv_spec = pl.BlockSpec((1, 1, N_MEM, 2 * d), lambda i, j: (layer, i, 0, 0))
    tail_specs = [kv_spec, _const_spec((1, d)), _const_spec((d, d)), _const_spec((d, d))]
    if mix is None:
        kern, in_specs, args = _xattn_kernel, [tile] + tail_specs, (h, kv, g, wq, wo)
    else:
        yc, yd, w_mix = mix
        half = pl.BlockSpec((1, ts, DIL_WIDTH), lambda i, j: (i, j, 0))
        kern = _mix_xattn_kernel
        in_specs = [tile, half, half, _const_spec((d, d))] + tail_specs
        args = (h, yc, yd, w_mix, kv, g, wq, wo)
    return pl.pallas_call(
        kern,
        grid=(b, s // ts),
        in_specs=in_specs,
        out_specs=tile,
        out_shape=jax.ShapeDtypeStruct(h.shape, F32),
        scratch_shapes=[
            pltpu.VMEM((ts, d), BF16),
            pltpu.VMEM((ts, d), BF16),
            pltpu.VMEM((XATTN_HEADS, ts // 2, N_MEM), F32),
            pltpu.VMEM((XATTN_HEADS, ts // 2, N_MEM), BF16),
        ] + ([] if mix is None else [pltpu.VMEM((ts, d), F32)]),
        compiler_params=_params("arbitrary", "arbitrary"),
        name="xattn" if mix is None else "mix_xattn",
    )(*args)


def _ffn_kernel(x_ref, g_ref, wgu_ref, wd_ref, fg_ref, o_ref, *, final_norm):
    x = x_ref[...]
    tm = x.shape[0]
    hn_rows, gate_rows = [], []
    for r in range(0, tm, tm // 2):
        hn_rows.append(_rms(x[r:r + tm // 2], g_ref[...]).astype(BF16))
        gate_rows.append(_dot(hn_rows[-1], wgu_ref[:, FF_CHUNKS[0][0]:FF_CHUNKS[0][1]]))
    hn = jnp.concatenate(hn_rows, axis=0)
    acc = x
    for c, (lo, hi) in enumerate(FF_CHUNKS):
        gate = jnp.concatenate(gate_rows, axis=0) if c == 0 else _dot(hn, wgu_ref[:, lo:hi])
        up = _dot(hn, wgu_ref[:, D_FF + lo:D_FF + hi])
        act = (_silu(gate) * up).astype(BF16)
        acc = acc + _dot(act, wd_ref[lo:hi, :])
    if final_norm:
        acc = _rms(acc, fg_ref[...])
    o_ref[...] = acc


def _ffn(h, g, wgu, wd, fg, final_norm):
    b, s, d = h.shape
    t = b * s
    tm = WIDE_TILE
    tile = pl.BlockSpec((tm, d), lambda i: (i, 0))
    out = pl.pallas_call(
        functools.partial(_ffn_kernel, final_norm=final_norm),
        grid=(t // tm,),
        in_specs=[tile, _const_spec((1, d)), _const_spec((d, 2 * D_FF)), _const_spec((D_FF, d)),
                  _const_spec((1, d))],
        out_specs=tile,
        out_shape=jax.ShapeDtypeStruct((t, d), F32),
        compiler_params=_params("arbitrary"),
        name="ffn_final" if final_norm else "ffn",
    )(h.reshape(t, d), g, wgu, wd, fg)
    return out.reshape(b, s, d)


def _odd_proj_kernel(x_ref, g_ref, w_ref, wkt_ref, qc_ref, qsa_ref, qsb_ref, dc_ref, dsa_ref, dsb_ref,
                     rc_ref, rs_ref, kct_ref, kst_ref,
                     cq_ref, ck_ref, cv_ref, rq_ref, rkt_ref, rv_ref, rg_ref):
    ts = x_ref.shape[1]
    hn_rows, first_rows = [], []
    for r in range(0, ts, ts // 2):
        hn_rows.append(_rms(x_ref[0, r:r + ts // 2, :], g_ref[...]).astype(BF16))
        first_rows.append(_dot(hn_rows[-1], w_ref[:, :DIL_WIDTH]))
    hn = jnp.concatenate(hn_rows, axis=0)

    def section(i):
        if i == 0:
            return jnp.concatenate(first_rows, axis=0)
        return _dot(hn, w_ref[:, i * DIL_WIDTH:(i + 1) * DIL_WIDTH])

    def lane_blocks(z, fn):
        return jnp.concatenate([fn(z[:, j * LANES:(j + 1) * LANES]) for j in range(z.shape[1] // LANES)], axis=1)

    half = ROPE_DIM // 2

    def dil_rot(c_ref, sa_ref, sb_ref):
        c, sa, sb = c_ref[...], sa_ref[...], sb_ref[...]
        return lambda zb: zb * c + pltpu.roll(zb, half, 1) * sa + pltpu.roll(zb, LANES - half, 1) * sb

    cq_ref[0] = lane_blocks(section(0), dil_rot(qc_ref, qsa_ref, qsb_ref))
    ck_ref[0] = lane_blocks(section(1), dil_rot(dc_ref, dsa_ref, dsb_ref))
    cv_ref[0] = section(2)

    rc, rs = rc_ref[...], rs_ref[...]

    def ret_rot(zb):
        return zb * rc + pltpu.roll(zb, RET_HEAD_DIM // 2, 1) * rs

    rq_ref[0] = lane_blocks(section(3), ret_rot).astype(BF16)
    rv_ref[0] = section(5).astype(BF16)
    rg_ref[0] = section(6)

    kt = _dot_nt(wkt_ref[...], hn)
    kct, kst = kct_ref[...], kst_ref[...]
    hd = RET_HEAD_DIM
    for h in range(RET_HEADS):
        zb = kt[h * hd:(h + 1) * hd, :]
        swapped = jnp.concatenate([zb[hd // 2:], zb[:hd // 2]], axis=0)
        rkt_ref[0, h * hd:(h + 1) * hd, :] = (zb * kct + swapped * kst).astype(BF16)


def _rotary_tables(s):
    pos = jnp.arange(s, dtype=jnp.int32).astype(F32)
    lane = jnp.arange(LANES)
    half = ROPE_DIM // 2
    inv = 1.0 / jnp.power(jnp.float32(ROPE_THETA), jnp.arange(half, dtype=F32) / half)
    ang = pos[:, None] * inv[None, :]
    cos, sin = jnp.cos(ang), jnp.sin(ang)
    hl = lane % DIL_HEAD_DIM
    cos_l, sin_l = cos[:, hl % half], sin[:, hl % half]
    dc = jnp.where(hl[None] < ROPE_DIM, cos_l, 1.0)
    dsa = jnp.where((hl[None] >= half) & (hl[None] < ROPE_DIM), sin_l, 0.0)
    dsb = jnp.where(hl[None] < half, -sin_l, 0.0)
    rhalf = RET_HEAD_DIM // 2
    rinv = 1.0 / jnp.power(jnp.float32(RET_THETA), jnp.arange(rhalf, dtype=F32) / rhalf)
    rang = pos[:, None] * rinv[None, :]
    rcos, rsin = jnp.cos(rang), jnp.sin(rang)
    rc = jnp.concatenate([rcos, rcos], axis=1)
    rs = jnp.concatenate([-rsin, rsin], axis=1)
    qs = (DIL_HEAD_DIM ** -0.5) * math.log2(math.e)
    ks = RET_HEAD_DIM ** -0.5
    return (dc * qs, dsa * qs, dsb * qs, dc, dsa, dsb, rc, rs), ((rc * ks).T, (rs * ks).T)


def _odd_proj(h, g, w_in):
    b, s, d = h.shape
    ts = SEQ_TILE
    tables, tables_t = _rotary_tables(s)
    wkt = w_in[:, 3 * DIL_WIDTH + RET_WIDTH:3 * DIL_WIDTH + 2 * RET_WIDTH].T
    tile = pl.BlockSpec((1, ts, d), lambda j, i: (i, j, 0))
    tab = pl.BlockSpec((ts, LANES), lambda j, i: (j, 0))
    tab_t = pl.BlockSpec((LANES, ts), lambda j, i: (0, j))
    out = pl.BlockSpec((1, ts, DIL_WIDTH), lambda j, i: (i, j, 0))
    out_t = pl.BlockSpec((1, RET_WIDTH, ts), lambda j, i: (i, 0, j))
    f32o = jax.ShapeDtypeStruct((b, s, DIL_WIDTH), F32)
    bf16o = jax.ShapeDtypeStruct((b, s, RET_WIDTH), BF16)
    return pl.pallas_call(
        _odd_proj_kernel,
        grid=(s // ts, b),
        in_specs=[tile, _const_spec((1, d)), _const_spec((d, ODD_IN)), _const_spec((RET_WIDTH, d))]
        + [tab] * len(tables) + [tab_t] * len(tables_t),
        out_specs=[out, out, out, out, out_t, out, out],
        out_shape=[f32o, f32o, f32o, bf16o, jax.ShapeDtypeStruct((b, RET_WIDTH, s), BF16), bf16o, f32o],
        compiler_params=_params("arbitrary", "arbitrary"),
        name="odd_proj",
    )(h, g, w_in, wkt, *tables, *tables_t)


def _dilated_kernel(q_ref, k_ref, v_ref, o_ref, cls_ref, far_ref, mrg_ref, sel_ref, bias_ref, s_ref, p_ref, mcur_ref):
    s_len = q_ref.shape[1]
    blk = DIL_BLOCK
    lane = lax.broadcasted_iota(jnp.int32, (blk, LANES), 1)
    first_head = lane < DIL_HEAD_DIM

    qi = lax.broadcasted_iota(jnp.int32, (2 * blk, blk), 1)
    kr = lax.broadcasted_iota(jnp.int32, (2 * blk, blk), 0)
    sel_ref[...] = jnp.where(kr % blk == qi, 1.0, 0.0).astype(BF16)
    bias_ref[0] = jnp.where(jnp.abs(qi + blk - kr - blk // 2) <= blk // 2, 0.0, -MASK_BIG).astype(BF16)
    bias_ref[1] = jnp.where(kr <= qi, 0.0, -MASK_BIG).astype(BF16)

    (_, d_near), (_, dm), (_, df) = DIL_PAIRS
    assert d_near == 1 and df % dm == 0 and all(w // d == blk for w, d in DIL_PAIRS)
    ratio, cls_len = df // dm, s_len // dm
    for i, ref in enumerate((q_ref, k_ref, v_ref)):
        for r in range(dm):
            cls_ref[i, r * cls_len:(r + 1) * cls_len, :] = ref[0, pl.ds(r, cls_len, stride=dm), :]

    blocks = []
    for r in range(dm):
        for c in range(ratio):
            for n in range(s_len // (df * blk)):
                start = r * cls_len + c + n * ratio * blk
                blocks.append(("far", start, ratio, n > 0, pl.ds(start, blk, stride=ratio)))
    for r in range(dm):
        for n in range(cls_len // blk):
            blocks.append(("mid", r * cls_len + n * blk, 1, n > 0, pl.ds(r + n * dm * blk, blk, stride=dm)))
    for n in range(s_len // blk):
        blocks.append(("near", n * blk, 1, n > 0, pl.ds(n * blk, blk)))

    def rows(t, prev=False):
        _, start, stride, _, _ = blocks[t]
        start -= stride * blk if prev else 0
        return pl.ds(start, blk) if stride == 1 else pl.ds(start, blk, stride=stride)

    def load(i, t, prev=False):
        if blocks[t][0] == "near":
            return (q_ref, k_ref, v_ref)[i][0, rows(t, prev), :]
        return cls_ref[i, rows(t, prev), :]

    def load2(i, t):
        x = load(i, t).astype(BF16)
        return jnp.concatenate([load(i, t, True).astype(BF16), x], axis=0) if blocks[t][3] else x

    def n_keys(t):
        return 2 * blk if blocks[t][3] else blk

    def merged(m1, acc1, den1, m2, acc2, den2):
        m = jnp.maximum(m1, m2)
        w1, w2 = jnp.exp2(m1 - m), jnp.exp2(m2 - m)
        return m, w1 * acc1 + w2 * acc2, w1 * den1 + w2 * den2

    def scores(t):
        q = load(0, t)
        q2 = jnp.concatenate([jnp.where(first_head, q, 0.0), jnp.where(first_head, 0.0, q)], axis=0).astype(BF16)
        bias = bias_ref[0] if blocks[t][3] else bias_ref[1, :blk, :]
        s_ref[t % DIL_SLOTS, :, :n_keys(t)] = _dot_nt(jnp.concatenate([q2, sel_ref[...]], axis=1),
                                                      jnp.concatenate([load2(1, t), bias], axis=1))

    def softmax(t):
        s = s_ref[t % DIL_SLOTS, :, :n_keys(t)]
        m = jnp.max(s, axis=-1, keepdims=True)
        p_ref[t % DIL_SLOTS, :, :n_keys(t)] = jnp.exp2(s - m).astype(BF16)
        mcur_ref[t % DIL_SLOTS] = jnp.where(first_head, m[:blk], m[blk:])

    def values(t):
        vext = jnp.concatenate([load2(2, t), jnp.ones((n_keys(t), LANES), BF16)], axis=1)
        o2 = _dot(p_ref[t % DIL_SLOTS, :, :n_keys(t)], vext)
        acc = jnp.where(first_head, o2[:blk, :LANES], o2[blk:, :LANES])
        den = jnp.where(first_head, o2[:blk, LANES:], o2[blk:, LANES:])
        m = mcur_ref[t % DIL_SLOTS]
        kind, dst = blocks[t][0], blocks[t][4]
        if kind == "far":
            far_ref[0, dst, :], far_ref[1, dst, :], far_ref[2, dst, :] = m, acc, den
        elif kind == "mid":
            src = rows(t)
            m, acc, den = merged(m, acc, den, far_ref[0, src, :], far_ref[1, src, :], far_ref[2, src, :])
            mrg_ref[0, dst, :], mrg_ref[1, dst, :], mrg_ref[2, dst, :] = m, acc, den
        else:
            _, acc, den = merged(m, acc, den, mrg_ref[0, dst, :], mrg_ref[1, dst, :], mrg_ref[2, dst, :])
            o_ref[0, dst, :] = (acc / den).astype(BF16)

    for t in range(len(blocks) + 2):
        if t < len(blocks):
            scores(t)
        if 0 <= t - 1 < len(blocks):
            softmax(t - 1)
        if 0 <= t - 2 < len(blocks):
            values(t - 2)


def _dilated_attention(cq, ck, cv):
    b, s, _ = cq.shape
    spec = pl.BlockSpec((1, s, LANES), lambda i, p: (i, 0, p))
    nbr = len(DIL_PAIRS)
    return pl.pallas_call(
        _dilated_kernel,
        grid=(b, DIL_WIDTH // LANES),
        in_specs=[spec] * 3,
        out_specs=spec,
        out_shape=jax.ShapeDtypeStruct((b, s, DIL_WIDTH), BF16),
        scratch_shapes=[pltpu.VMEM((3, s, LANES), F32)] * 3 + [
            pltpu.VMEM((2 * DIL_BLOCK, LANES), BF16),
            pltpu.VMEM((2, 2 * DIL_BLOCK, LANES), BF16),
            pltpu.VMEM((DIL_SLOTS, 2 * DIL_BLOCK, 2 * DIL_BLOCK), F32),
            pltpu.VMEM((DIL_SLOTS, 2 * DIL_BLOCK, 2 * DIL_BLOCK), BF16),
            pltpu.VMEM((DIL_SLOTS, DIL_BLOCK, LANES), F32),
        ],
        compiler_params=_params("arbitrary", "arbitrary"),
        name="dilated_attention",
    )(cq, ck, cv)


def _retention_kernel(q_ref, kt_ref, v_ref, g_ref, dec_ref, kdec_ref, qdec_ref, o_ref,
                      y_ref, kv_ref, st_ref, s_ref, in_ref):
    s_len = q_ref.shape[1]
    c_len = RET_CHUNK
    n_chunks = s_len // c_len
    decay = dec_ref[0]
    kdec = kdec_ref[0]
    qdec = qdec_ref[0]
    g_chunk = qdec[c_len - 1:c_len, :]

    def chunk(c):
        return slice(c * c_len, (c + 1) * c_len)

    def decay_scores(c):
        in_ref[c] = (s_ref[c] * decay).astype(BF16)

    for c in range(n_chunks + 1):
        if c < n_chunks:
            ktc = kt_ref[0, :, chunk(c)]
            s_ref[c] = _dot(q_ref[0, chunk(c), :], ktc)
            kv_ref[c] = _dot((ktc.astype(F32) * kdec).astype(BF16), v_ref[0, chunk(c), :])
        if c >= 1:
            decay_scores(c - 1)
    state = jnp.zeros((RET_HEAD_DIM, RET_HEAD_DIM), F32)
    for c in range(n_chunks):
        st_ref[c] = state.astype(BF16)
        state = state * g_chunk + kv_ref[c]

    def outputs(c):
        y_ref[chunk(c), :] = (_dot(in_ref[c], v_ref[0, chunk(c), :])
                              + _dot(q_ref[0, chunk(c), :], st_ref[c]) * qdec)

    def normalise(c):
        y = y_ref[chunk(c), :]
        mu = jnp.mean(y, axis=-1, keepdims=True)
        yc = y - mu
        var = jnp.mean(yc * yc, axis=-1, keepdims=True)
        o_ref[0, chunk(c), :] = (_silu(g_ref[0, chunk(c), :]) * (yc * lax.rsqrt(var + EPS))).astype(BF16)

    for c in range(n_chunks + 1):
        if c < n_chunks:
            outputs(c)
        if c >= 1:
            normalise(c - 1)


def _retention_tables():
    c_len = RET_CHUNK
    gamma = 1.0 - jnp.power(2.0, -5.0 - jnp.arange(RET_HEADS, dtype=F32))
    log_g = jnp.log(gamma)
    idx = jnp.arange(c_len, dtype=F32)
    rel = idx[:, None] - idx[None, :]
    decay = jnp.where(rel[None] >= 0, jnp.exp(jnp.maximum(rel, 0.0)[None] * log_g[:, None, None]), 0.0)
    k_decay = jnp.exp((c_len - 1 - idx)[None, :] * log_g[:, None])
    q_decay = jnp.exp((idx + 1)[None, :] * log_g[:, None])
    return (decay, jnp.broadcast_to(k_decay[:, None, :], (RET_HEADS, RET_HEAD_DIM, c_len)),
            jnp.broadcast_to(q_decay[:, :, None], (RET_HEADS, c_len, RET_HEAD_DIM)))


def _retention(rq, rkt, rv, rg):
    b, s, _ = rq.shape
    hd = RET_HEAD_DIM
    spec = pl.BlockSpec((1, s, hd), lambda i, h: (i, 0, h))
    spec_t = pl.BlockSpec((1, hd, s), lambda i, h: (i, h, 0))
    tables = _retention_tables()
    return pl.pallas_call(
        _retention_kernel,
        grid=(b, RET_HEADS),
        in_specs=[spec, spec_t, spec, spec]
        + [pl.BlockSpec((1,) + t.shape[1:], lambda i, h: (h, 0, 0)) for t in tables],
        out_specs=spec,
        out_shape=jax.ShapeDtypeStruct((b, s, RET_WIDTH), BF16),
        scratch_shapes=[
            pltpu.VMEM((s, hd), F32),
            pltpu.VMEM((s // RET_CHUNK, hd, hd), F32),
            pltpu.VMEM((s // RET_CHUNK, hd, hd), BF16),
            pltpu.VMEM((s // RET_CHUNK, RET_CHUNK, RET_CHUNK), F32),
            pltpu.VMEM((s // RET_CHUNK, RET_CHUNK, RET_CHUNK), BF16),
        ],
        compiler_params=_params("arbitrary", "arbitrary"),
        name="retention",
    )(rq, rkt, rv, rg, *tables)


def kernel(x, mem, even_mix_norm, even_w_in, pool_w, pool_scale, sgu_norm, sgu_w, sgu_b, even_w_out,
           odd_mix_norm, odd_w_in, odd_w_out, xattn_norm, mem_norm, xattn_wq, xattn_wkv, xattn_wo,
           ffn_norm, ffn_w_gate_up, ffn_w_down, final_norm):
    depth = xattn_wq.shape[0]
    assert x.shape[1] % (max(d for _, d in DIL_PAIRS) * DIL_BLOCK) == 0 and x.shape[2] == D_MODEL

    def row(v):
        return v.reshape(1, -1)

    kv = _memory_kv(mem, mem_norm, xattn_wkv.astype(BF16))
    h = x
    for layer in range(depth):
        i = layer // 2
        if layer % 2 == 0:
            h = _even_mixer(h, row(even_mix_norm[i]), even_w_in[i].astype(BF16), pool_w[i].astype(BF16),
                            row(pool_scale[i]), row(sgu_norm[i]), sgu_w[i], sgu_b[i].T,
                            even_w_out[i].astype(BF16))
            mix = None
        else:
            cq, ck, cv, rq, rk, rv, rg = _odd_proj(h, row(odd_mix_norm[i]), odd_w_in[i].astype(BF16))
            yc = _dilated_attention(cq, ck, cv)
            yd = _retention(rq, rk, rv, rg)
            mix = (yc, yd, odd_w_out[i].astype(BF16))
        h = _xattn(h, kv, layer, row(xattn_norm[layer]), xattn_wq[layer].astype(BF16),
                   xattn_wo[layer].astype(BF16), mix)
        h = _ffn(h, row(ffn_norm[layer]), ffn_w_gate_up[layer].astype(BF16), ffn_w_down[layer].astype(BF16),
                 row(final_norm), final_norm=layer == depth - 1)
    return h
```

```python
import functools
import math

import jax
import jax.numpy as jnp
from jax import lax
from jax.experimental import pallas as pl
from jax.experimental.pallas import tpu as pltpu

F32 = jnp.float32
BF16 = jnp.bfloat16

EPS = 1e-6
D_MODEL = 1024
N_MEM = 256

POOL_WINDOWS = (2, 4, 8, 16)
POOL_HALO = 8 * len(POOL_WINDOWS)
assert POOL_WINDOWS == tuple(2 ** (g + 1) for g in range(len(POOL_WINDOWS)))
POOL_WIDTH = 512
GROUP_DIM = 128
SGU_WIDTH = 512
SGU_CHUNK = 128
EVEN_IN = POOL_WIDTH + 2 * SGU_WIDTH

DIL_HEADS = 8
DIL_HEAD_DIM = 64
DIL_WIDTH = 512
DIL_PAIRS = ((128, 1), (512, 4), (2048, 16))
DIL_BLOCK = 128
DIL_SLOTS = 3
MASK_BIG = 2.0 ** 126
ROPE_THETA = 500000.0
ROPE_DIM = 16
RET_HEADS = 4
RET_HEAD_DIM = 128
RET_WIDTH = 512
RET_HEADS_PER_STEP = 2
RET_CHUNK = 256
RET_THETA = 10000.0
ODD_IN = 3 * DIL_WIDTH + 4 * RET_WIDTH

XATTN_HEADS = 4
XATTN_HEAD_DIM = 256
D_FF = 2816

LANES = 128
MXU_DIM = 256
FF_STEP = 6 * MXU_DIM
FF_CHUNKS = tuple((lo, min(lo + FF_STEP, D_FF)) for lo in range(0, D_FF, FF_STEP))
SEQ_TILE = 512
WIDE_TILE = 1024
VMEM_LIMIT = 56 * 1024 * 1024


def _params(*sem):
    return pltpu.CompilerParams(dimension_semantics=sem, vmem_limit_bytes=VMEM_LIMIT)


def _const_spec(shape):
    zeros = (0,) * len(shape)
    return pl.BlockSpec(shape, lambda *_: zeros, pipeline_mode=pl.Buffered(1))


def _rms(x, g):
    ms = jnp.mean(x * x, axis=-1, keepdims=True)
    return x * lax.rsqrt(ms + EPS) * g


def _gelu_tanh(x):
    return 0.5 * x * (1.0 + jnp.tanh(math.sqrt(2.0 / math.pi) * (x + 0.044715 * (x * x * x))))


def _silu(x):
    return x * (1.0 / (1.0 + jnp.exp(-x)))


def _dot(a, b):
    return jnp.dot(a, b, preferred_element_type=F32)


def _dot_nt(a, b):
    return lax.dot_general(a, b, (((1,), (1,)), ((), ())), preferred_element_type=F32)


def _even_kernel(x_ref, g_ref, win_ref, pw_ref, ps_ref, sn_ref, sw_ref, sb_ref, wout_ref,
                 o_ref, ext_ref, lva_ref, lvb_ref, u_ref, v_ref, ycat_ref):
    si = pl.program_id(1)
    ts = x_ref.shape[1]
    halo, end = POOL_HALO, POOL_HALO + ts
    half = MXU_DIM

    @pl.when(si == 0)
    def _():
        ext_ref[0:halo, :] = jnp.zeros((halo, POOL_WIDTH), F32)

    hn_rows = []
    for r in range(0, ts, ts // 2):
        hn_rows.append(_rms(x_ref[0, r:r + ts // 2, :], g_ref[...]).astype(BF16))
        ext_ref[halo + r:halo + r + ts // 2, :] = _dot(hn_rows[-1], win_ref[:, :POOL_WIDTH])
    hn = jnp.concatenate(hn_rows, axis=0)

    def in_proj(ref, c0, c1):
        ref[:, c0 - c1:c0 - c1 + half] = _dot(hn, win_ref[:, c0:c0 + half])

    def gelu_inplace(ref, c):
        ref[:, c:c + half] = _gelu_tanh(ref[:, c:c + half])

    bufs = (lva_ref, lvb_ref)

    def level(k):
        src = ext_ref if k == 1 else bufs[k % 2]
        dst, lo, back, c0 = bufs[(k - 1) % 2], 8 * k, 2 ** (k - 1), (k - 1) * GROUP_DIM
        dst[lo:end, c0:] = src[lo:end, c0:] + src[lo - back:end - back, c0:]

    u0, v0 = POOL_WIDTH, POOL_WIDTH + SGU_WIDTH
    in_proj(u_ref, u0, u0)
    level(1)
    level(2)
    in_proj(u_ref, u0 + half, u0)
    level(3)
    level(4)
    assert len(POOL_WINDOWS) == 4

    in_proj(v_ref, v0, v0)
    pos = si * ts + lax.broadcasted_iota(jnp.int32, (ts, 1), 0)
    for g, win in enumerate(POOL_WINDOWS):
        cols = slice(g * GROUP_DIM, (g + 1) * GROUP_DIM)
        wsum = bufs[g % 2][halo:end, cols]
        cnt = jnp.minimum(pos + 1, win).astype(F32)
        d = (wsum / cnt - ext_ref[halo:end, cols]).astype(BF16)
        ya = _dot(d, pw_ref[g]) * ps_ref[:, cols]
        ycat_ref[:, cols] = ya.astype(BF16)
    ext_ref[0:halo, :] = ext_ref[ts:end, :]

    in_proj(v_ref, v0 + half, v0)
    gelu_inplace(u_ref, 0)
    gelu_inplace(u_ref, half)

    acc = x_ref[0] + _dot(ycat_ref[:, :POOL_WIDTH], wout_ref[:POOL_WIDTH, :])
    gelu_inplace(v_ref, 0)
    gelu_inplace(v_ref, half)
    vn = _rms(v_ref[...], sn_ref[...]).astype(BF16)

    nc = ts // SGU_CHUNK
    row = lax.broadcasted_iota(jnp.int32, (SGU_CHUNK, SGU_CHUNK), 0)
    col = lax.broadcasted_iota(jnp.int32, (SGU_CHUNK, SGU_CHUNK), 1)
    groups_per_tile = MXU_DIM // GROUP_DIM
    for g in range(SGU_WIDTH // GROUP_DIM):
        cols = slice(g * GROUP_DIM, (g + 1) * GROUP_DIM)
        ws = jnp.where(row >= col, sw_ref[g], 0.0).astype(BF16)
        rhs = jnp.concatenate([vn[c * SGU_CHUNK:(c + 1) * SGU_CHUNK, cols] for c in range(nc)], axis=1)
        mixed = _dot(ws, rhs) + sb_ref[:, g:g + 1]
        for c in range(nc):
            rows = slice(c * SGU_CHUNK, (c + 1) * SGU_CHUNK)
            yb = u_ref[rows, cols] * mixed[:, c * SGU_CHUNK:(c + 1) * SGU_CHUNK]
            ycat_ref[rows, POOL_WIDTH + g * GROUP_DIM:POOL_WIDTH + (g + 1) * GROUP_DIM] = yb.astype(BF16)
        if (g + 1) % groups_per_tile == 0:
            k0 = POOL_WIDTH + (g + 1 - groups_per_tile) * GROUP_DIM
            acc = acc + _dot(ycat_ref[:, k0:k0 + MXU_DIM], wout_ref[k0:k0 + MXU_DIM, :])
    o_ref[0] = acc


def _even_mixer(h, g, w_in, pool_w, pool_scale, sgu_norm, sgu_w, sgu_b_t, w_out):
    b, s, d = h.shape
    ts = WIDE_TILE
    return pl.pallas_call(
        _even_kernel,
        grid=(b, s // ts),
        in_specs=[
            pl.BlockSpec((1, ts, d), lambda i, j: (i, j, 0)),
            _const_spec((1, d)),
            _const_spec((d, EVEN_IN)),
            _const_spec(pool_w.shape),
            _const_spec((1, POOL_WIDTH)),
            _const_spec((1, SGU_WIDTH)),
            _const_spec(sgu_w.shape),
            _const_spec(sgu_b_t.shape),
            _const_spec((POOL_WIDTH + SGU_WIDTH, d)),
        ],
        out_specs=pl.BlockSpec((1, ts, d), lambda i, j: (i, j, 0)),
        out_shape=jax.ShapeDtypeStruct(h.shape, F32),
        scratch_shapes=[pltpu.VMEM((POOL_HALO + ts, POOL_WIDTH), F32)] * 3 + [
            pltpu.VMEM((ts, SGU_WIDTH), F32),
            pltpu.VMEM((ts, SGU_WIDTH), F32),
            pltpu.VMEM((ts, POOL_WIDTH + SGU_WIDTH), BF16),
        ],
        compiler_params=_params("arbitrary", "arbitrary"),
        name="even_mixer",
    )(h, g, w_in, pool_w, pool_scale, sgu_norm, sgu_w, sgu_b_t, w_out)


def _kv_kernel(m_ref, g_ref, w_ref, o_ref):
    mn = _rms(m_ref[...], g_ref[0]).astype(BF16)
    o_ref[0] = _dot(mn, w_ref[0]).astype(BF16)


def _memory_kv(mem, mem_norm, wkv):
    b, m, d = mem.shape
    depth = wkv.shape[0]
    rows = b * m
    tm = math.gcd(rows, 2 * SEQ_TILE)
    out = pl.pallas_call(
        _kv_kernel,
        grid=(depth, rows // tm),
        in_specs=[
            pl.BlockSpec((tm, d), lambda l, i: (i, 0)),
            pl.BlockSpec((1, 1, d), lambda l, i: (l, 0, 0)),
            pl.BlockSpec((1, d, 2 * d), lambda l, i: (l, 0, 0)),
        ],
        out_specs=pl.BlockSpec((1, tm, 2 * d), lambda l, i: (l, i, 0)),
        out_shape=jax.ShapeDtypeStruct((depth, rows, 2 * d), BF16),
        compiler_params=_params("arbitrary", "arbitrary"),
        name="memory_kv",
    )(mem.reshape(rows, d), mem_norm.reshape(depth, 1, d), wkv)
    return out.reshape(depth, b, m, 2 * d)


def _xattn_body(x_ref, kv_ref, g_ref, wq_ref, wo_ref, o_ref, q_ref, ocat_ref, s_ref, p_ref):
    ts = x_ref.shape[0]
    top, bot = slice(0, ts // 2), slice(ts // 2, ts)
    q_scale = (XATTN_HEAD_DIM ** -0.5) * math.log2(math.e)

    def cols(h):
        return slice(h * XATTN_HEAD_DIM, (h + 1) * XATTN_HEAD_DIM)

    def q_proj(rows):
        hn = _rms(x_ref[rows, :], g_ref[...]).astype(BF16)
        q_ref[rows, :] = (_dot(hn, wq_ref[...]) * q_scale).astype(BF16)

    def scores(rows):
        for h in range(XATTN_HEADS):
            s_ref[h] = _dot_nt(q_ref[rows, cols(h)], kv_ref[0, 0, :, cols(h)])

    def softmax():
        for h in range(XATTN_HEADS):
            s = s_ref[h]
            p = jnp.exp2(s - jnp.max(s, axis=-1, keepdims=True))
            p_ref[h] = (p * (1.0 / jnp.sum(p, axis=-1, keepdims=True))).astype(BF16)

    def values(rows):
        for h in range(XATTN_HEADS):
            vh = kv_ref[0, 0, :, D_MODEL + h * XATTN_HEAD_DIM:D_MODEL + (h + 1) * XATTN_HEAD_DIM]
            ocat_ref[rows, cols(h)] = _dot(p_ref[h], vh).astype(BF16)

    def out_proj(rows):
        o_ref[0, rows, :] = x_ref[rows, :] + _dot(ocat_ref[rows, :], wo_ref[...])

    q_proj(top)
    scores(top)
    q_proj(bot)
    softmax()
    values(top)
    scores(bot)
    out_proj(top)
    softmax()
    values(bot)
    out_proj(bot)


def _xattn_kernel(h_ref, kv_ref, g_ref, wq_ref, wo_ref, o_ref, q_ref, ocat_ref, s_ref, p_ref):
    _xattn_body(h_ref.at[0], kv_ref, g_ref, wq_ref, wo_ref, o_ref, q_ref, ocat_ref, s_ref, p_ref)


def _mix_xattn_kernel(h_ref, yc_ref, yd_ref, wmix_ref, kv_ref, g_ref, wq_ref, wo_ref, o_ref,
                      q_ref, ocat_ref, s_ref, p_ref, x_ref):
    ts = h_ref.shape[1]
    for r in range(0, ts, ts // 2):
        rows = slice(r, r + ts // 2)
        x_ref[rows, :] = (h_ref[0, rows, :] + _dot(yc_ref[0, rows, :], wmix_ref[:DIL_WIDTH, :])
                          + _dot(yd_ref[0, rows, :], wmix_ref[DIL_WIDTH:, :]))
    _xattn_body(x_ref, kv_ref, g_ref, wq_ref, wo_ref, o_ref, q_ref, ocat_ref, s_ref, p_ref)


def _xattn(h, kv, layer, g, wq, wo, mix=None):
    b, s, d = h.shape
    ts = WIDE_TILE
    tile = pl.BlockSpec((1, ts, d), lambda i, j: (i, j, 0))
    kv_spec = pl.BlockSpec((1, 1, N_MEM, 2 * d), lambda i, j: (layer, i, 0, 0))
    tail_specs = [kv_spec, _const_spec((1, d)), _const_spec((d, d)), _const_spec((d, d))]
    if mix is None:
        kern, in_specs, args = _xattn_kernel, [tile] + tail_specs, (h, kv, g, wq, wo)
    else:
        yc, yd, w_mix = mix
        half = pl.BlockSpec((1, ts, DIL_WIDTH), lambda i, j: (i, j, 0))
        kern = _mix_xattn_kernel
        in_specs = [tile, half, half, _const_spec((d, d))] + tail_specs
        args = (h, yc, yd, w_mix, kv, g, wq, wo)
    return pl.pallas_call(
        kern,
        grid=(b, s // ts),
        in_specs=in_specs,
        out_specs=tile,
        out_shape=jax.ShapeDtypeStruct(h.shape, F32),
        scratch_shapes=[
            pltpu.VMEM((ts, d), BF16),
            pltpu.VMEM((ts, d), BF16),
            pltpu.VMEM((XATTN_HEADS, ts // 2, N_MEM), F32),
            pltpu.VMEM((XATTN_HEADS, ts // 2, N_MEM), BF16),
        ] + ([] if mix is None else [pltpu.VMEM((ts, d), F32)]),
        compiler_params=_params("arbitrary", "arbitrary"),
        name="xattn" if mix is None else "mix_xattn",
    )(*args)


def _ffn_kernel(x_ref, g_ref, wgu_ref, wd_ref, fg_ref, o_ref, *, final_norm):
    tm = x_ref.shape[0]
    hn_rows, gate_rows = [], []
    for r in range(0, tm, tm // 2):
        hn_rows.append(_rms(x_ref[r:r + tm // 2, :], g_ref[...]).astype(BF16))
        gate_rows.append(_dot(hn_rows[-1], wgu_ref[:, FF_CHUNKS[0][0]:FF_CHUNKS[0][1]]))
    hn = jnp.concatenate(hn_rows, axis=0)
    acc = None
    for c, (lo, hi) in enumerate(FF_CHUNKS):
        gate = jnp.concatenate(gate_rows, axis=0) if c == 0 else _dot(hn, wgu_ref[:, lo:hi])
        up = _dot(hn, wgu_ref[:, D_FF + lo:D_FF + hi])
        act = (_silu(gate) * up).astype(BF16)
        acc = (x_ref[...] if acc is None else acc) + _dot(act, wd_ref[lo:hi, :])
    if final_norm:
        acc = _rms(acc, fg_ref[...])
    o_ref[...] = acc


def _ffn(h, g, wgu, wd, fg, final_norm):
    b, s, d = h.shape
    t = b * s
    tm = WIDE_TILE
    tile = pl.BlockSpec((tm, d), lambda i: (i, 0))
    out = pl.pallas_call(
        functools.partial(_ffn_kernel, final_norm=final_norm),
        grid=(t // tm,),
        in_specs=[tile, _const_spec((1, d)), _const_spec((d, 2 * D_FF)), _const_spec((D_FF, d)),
                  _const_spec((1, d))],
        out_specs=tile,
        out_shape=jax.ShapeDtypeStruct((t, d), F32),
        compiler_params=_params("arbitrary"),
        name="ffn_final" if final_norm else "ffn",
    )(h.reshape(t, d), g, wgu, wd, fg)
    return out.reshape(b, s, d)


def _odd_proj_kernel(x_ref, g_ref, w_ref, wkt_ref, qc_ref, qsa_ref, qsb_ref, dc_ref, dsa_ref, dsb_ref,
                     rc_ref, rs_ref, kct_ref, kst_ref,
                     cq_ref, ck_ref, cv_ref, rq_ref, rkt_ref, rv_ref, rg_ref):
    ts = x_ref.shape[1]
    hn_rows, first_rows = [], []
    for r in range(0, ts, ts // 2):
        hn_rows.append(_rms(x_ref[0, r:r + ts // 2, :], g_ref[...]).astype(BF16))
        first_rows.append(_dot(hn_rows[-1], w_ref[:, :DIL_WIDTH]))
    hn = jnp.concatenate(hn_rows, axis=0)

    def section(i):
        if i == 0:
            return jnp.concatenate(first_rows, axis=0)
        return _dot(hn, w_ref[:, i * DIL_WIDTH:(i + 1) * DIL_WIDTH])

    def lane_blocks(z, fn):
        return jnp.concatenate([fn(z[:, j * LANES:(j + 1) * LANES]) for j in range(z.shape[1] // LANES)], axis=1)

    half = ROPE_DIM // 2

    def dil_rot(c_ref, sa_ref, sb_ref):
        c, sa, sb = c_ref[...], sa_ref[...], sb_ref[...]
        return lambda zb: zb * c + pltpu.roll(zb, half, 1) * sa + pltpu.roll(zb, LANES - half, 1) * sb

    cq_ref[0] = lane_blocks(section(0), dil_rot(qc_ref, qsa_ref, qsb_ref))
    ck_ref[0] = lane_blocks(section(1), dil_rot(dc_ref, dsa_ref, dsb_ref))
    cv_ref[0] = section(2)

    rc, rs = rc_ref[...], rs_ref[...]

    def ret_rot(zb):
        return zb * rc + pltpu.roll(zb, RET_HEAD_DIM // 2, 1) * rs

    rq_ref[0] = lane_blocks(section(3), ret_rot).astype(BF16)

    kt = _dot_nt(wkt_ref[...], hn)
    kct, kst = kct_ref[...], kst_ref[...]
    hd = RET_HEAD_DIM
    for h in range(RET_HEADS):
        zb = kt[h * hd:(h + 1) * hd, :]
        swapped = jnp.concatenate([zb[hd // 2:], zb[:hd // 2]], axis=0)
        rkt_ref[0, h * hd:(h + 1) * hd, :] = (zb * kct + swapped * kst).astype(BF16)

    rv_ref[0] = section(5).astype(BF16)
    rg_ref[0] = section(6)


def _rotary_tables(s):
    pos = jnp.arange(s, dtype=jnp.int32).astype(F32)
    lane = jnp.arange(LANES)
    half = ROPE_DIM // 2
    inv = 1.0 / jnp.power(jnp.float32(ROPE_THETA), jnp.arange(half, dtype=F32) / half)
    ang = pos[:, None] * inv[None, :]
    cos, sin = jnp.cos(ang), jnp.sin(ang)
    hl = lane % DIL_HEAD_DIM
    cos_l, sin_l = cos[:, hl % half], sin[:, hl % half]
    dc = jnp.where(hl[None] < ROPE_DIM, cos_l, 1.0)
    dsa = jnp.where((hl[None] >= half) & (hl[None] < ROPE_DIM), sin_l, 0.0)
    dsb = jnp.where(hl[None] < half, -sin_l, 0.0)
    rhalf = RET_HEAD_DIM // 2
    rinv = 1.0 / jnp.power(jnp.float32(RET_THETA), jnp.arange(rhalf, dtype=F32) / rhalf)
    rang = pos[:, None] * rinv[None, :]
    rcos, rsin = jnp.cos(rang), jnp.sin(rang)
    rc = jnp.concatenate([rcos, rcos], axis=1)
    rs = jnp.concatenate([-rsin, rsin], axis=1)
    qs = (DIL_HEAD_DIM ** -0.5) * math.log2(math.e)
    ks = RET_HEAD_DIM ** -0.5
    return (dc * qs, dsa * qs, dsb * qs, dc, dsa, dsb, rc, rs), ((rc * ks).T, (rs * ks).T)


def _odd_proj(h, g, w_in):
    b, s, d = h.shape
    ts = SEQ_TILE
    tables, tables_t = _rotary_tables(s)
    wkt = w_in[:, 3 * DIL_WIDTH + RET_WIDTH:3 * DIL_WIDTH + 2 * RET_WIDTH].T
    tile = pl.BlockSpec((1, ts, d), lambda j, i: (i, j, 0))
    tab = pl.BlockSpec((ts, LANES), lambda j, i: (j, 0))
    tab_t = pl.BlockSpec((LANES, ts), lambda j, i: (0, j))
    out = pl.BlockSpec((1, ts, DIL_WIDTH), lambda j, i: (i, j, 0))
    out_t = pl.BlockSpec((1, RET_WIDTH, ts), lambda j, i: (i, 0, j))
    f32o = jax.ShapeDtypeStruct((b, s, DIL_WIDTH), F32)
    bf16o = jax.ShapeDtypeStruct((b, s, RET_WIDTH), BF16)
    return pl.pallas_call(
        _odd_proj_kernel,
        grid=(s // ts, b),
        in_specs=[tile, _const_spec((1, d)), _const_spec((d, ODD_IN)), _const_spec((RET_WIDTH, d))]
        + [tab] * len(tables) + [tab_t] * len(tables_t),
        out_specs=[out, out, out, out, out_t, out, out],
        out_shape=[f32o, f32o, f32o, bf16o, jax.ShapeDtypeStruct((b, RET_WIDTH, s), BF16), bf16o, f32o],
        compiler_params=_params("arbitrary", "arbitrary"),
        name="odd_proj",
    )(h, g, w_in, wkt, *tables, *tables_t)


def _dilated_kernel(q_ref, k_ref, v_ref, o_ref, cls_ref, far_ref, mrg_ref, sel_ref, bias_ref, s_ref, p_ref, mcur_ref):
    s_len = q_ref.shape[1]
    blk = DIL_BLOCK
    lane = lax.broadcasted_iota(jnp.int32, (blk, LANES), 1)
    first_head = lane < DIL_HEAD_DIM

    qi = lax.broadcasted_iota(jnp.int32, (2 * blk, blk), 1)
    kr = lax.broadcasted_iota(jnp.int32, (2 * blk, blk), 0)
    sel_ref[...] = jnp.where(kr % blk == qi, 1.0, 0.0).astype(BF16)
    bias_ref[0] = jnp.where(jnp.abs(qi + blk - kr - blk // 2) <= blk // 2, 0.0, -MASK_BIG).astype(BF16)
    bias_ref[1] = jnp.where(kr <= qi, 0.0, -MASK_BIG).astype(BF16)

    (_, d_near), (_, dm), (_, df) = DIL_PAIRS
    assert d_near == 1 and df % dm == 0 and all(w // d == blk for w, d in DIL_PAIRS)
    ratio, cls_len = df // dm, s_len // dm
    for i, ref in enumerate((q_ref, k_ref, v_ref)):
        for r in range(dm):
            cls_ref[i, r * cls_len:(r + 1) * cls_len, :] = ref[0, pl.ds(r, cls_len, stride=dm), :]

    blocks = []
    for r in range(dm):
        for c in range(ratio):
            for n in range(s_len // (df * blk)):
                start = r * cls_len + c + n * ratio * blk
                blocks.append(("far", start, ratio, n > 0, pl.ds(start, blk, stride=ratio)))
    for r in range(dm):
        for n in range(cls_len // blk):
            blocks.append(("mid", r * cls_len + n * blk, 1, n > 0, pl.ds(r + n * dm * blk, blk, stride=dm)))
    for n in range(s_len // blk):
        blocks.append(("near", n * blk, 1, n > 0, pl.ds(n * blk, blk)))

    def rows(t, prev=False):
        _, start, stride, _, _ = blocks[t]
        start -= stride * blk if prev else 0
        return pl.ds(start, blk) if stride == 1 else pl.ds(start, blk, stride=stride)

    def load(i, t, prev=False):
        if blocks[t][0] == "near":
            return (q_ref, k_ref, v_ref)[i][0, rows(t, prev), :]
        return cls_ref[i, rows(t, prev), :]

    def load2(i, t):
        x = load(i, t).astype(BF16)
        return jnp.concatenate([load(i, t, True).astype(BF16), x], axis=0) if blocks[t][3] else x

    def n_keys(t):
        return 2 * blk if blocks[t][3] else blk

    def merged(m1, acc1, den1, m2, acc2, den2):
        m = jnp.maximum(m1, m2)
        w1, w2 = jnp.exp2(m1 - m), jnp.exp2(m2 - m)
        return m, w1 * acc1 + w2 * acc2, w1 * den1 + w2 * den2

    def scores(t):
        q = load(0, t)
        q2 = jnp.concatenate([jnp.where(first_head, q, 0.0), jnp.where(first_head, 0.0, q)], axis=0).astype(BF16)
        bias = bias_ref[0] if blocks[t][3] else bias_ref[1, :blk, :]
        s_ref[t % DIL_SLOTS, :, :n_keys(t)] = _dot_nt(jnp.concatenate([q2, sel_ref[...]], axis=1),
                                                      jnp.concatenate([load2(1, t), bias], axis=1))

    def softmax(t):
        s = s_ref[t % DIL_SLOTS, :, :n_keys(t)]
        m = jnp.max(s, axis=-1, keepdims=True)
        p_ref[t % DIL_SLOTS, :, :n_keys(t)] = jnp.exp2(s - m).astype(BF16)
        mcur_ref[t % DIL_SLOTS] = jnp.where(first_head, m[:blk], m[blk:])

    def values(t):
        vext = jnp.concatenate([load2(2, t), jnp.ones((n_keys(t), LANES), BF16)], axis=1)
        o2 = _dot(p_ref[t % DIL_SLOTS, :, :n_keys(t)], vext)
        acc = jnp.where(first_head, o2[:blk, :LANES], o2[blk:, :LANES])
        den = jnp.where(first_head, o2[:blk, LANES:], o2[blk:, LANES:])
        m = mcur_ref[t % DIL_SLOTS]
        kind, dst = blocks[t][0], blocks[t][4]
        if kind == "far":
            far_ref[0, dst, :], far_ref[1, dst, :], far_ref[2, dst, :] = m, acc, den
        elif kind == "mid":
            src = rows(t)
            m, acc, den = merged(m, acc, den, far_ref[0, src, :], far_ref[1, src, :], far_ref[2, src, :])
            mrg_ref[0, dst, :], mrg_ref[1, dst, :], mrg_ref[2, dst, :] = m, acc, den
        else:
            _, acc, den = merged(m, acc, den, mrg_ref[0, dst, :], mrg_ref[1, dst, :], mrg_ref[2, dst, :])
            o_ref[0, dst, :] = (acc / den).astype(BF16)

    for t in range(len(blocks) + 2):
        if t < len(blocks):
            scores(t)
        if 0 <= t - 1 < len(blocks):
            softmax(t - 1)
        if 0 <= t - 2 < len(blocks):
            values(t - 2)


def _dilated_attention(cq, ck, cv):
    b, s, _ = cq.shape
    spec = pl.BlockSpec((1, s, LANES), lambda i, p: (i, 0, p))
    nbr = len(DIL_PAIRS)
    return pl.pallas_call(
        _dilated_kernel,
        grid=(b, DIL_WIDTH // LANES),
        in_specs=[spec] * 3,
        out_specs=spec,
        out_shape=jax.ShapeDtypeStruct((b, s, DIL_WIDTH), BF16),
        scratch_shapes=[pltpu.VMEM((3, s, LANES), F32)] * 3 + [
            pltpu.VMEM((2 * DIL_BLOCK, LANES), BF16),
            pltpu.VMEM((2, 2 * DIL_BLOCK, LANES), BF16),
            pltpu.VMEM((DIL_SLOTS, 2 * DIL_BLOCK, 2 * DIL_BLOCK), F32),
            pltpu.VMEM((DIL_SLOTS, 2 * DIL_BLOCK, 2 * DIL_BLOCK), BF16),
            pltpu.VMEM((DIL_SLOTS, DIL_BLOCK, LANES), F32),
        ],
        compiler_params=_params("arbitrary", "arbitrary"),
        name="dilated_attention",
    )(cq, ck, cv)


def _retention_kernel(q_ref, kt_ref, v_ref, g_ref, dec_ref, kdec_ref, qdec_ref, o_ref,
                      y_ref, kv_ref, st_ref, s_ref, in_ref):
    s_len = q_ref.shape[1]
    hd = RET_HEAD_DIM
    c_len = RET_CHUNK
    n_chunks = s_len // c_len
    items = [(h, c) for c in range(n_chunks) for h in range(q_ref.shape[2] // hd)]

    def chunk(c):
        return slice(c * c_len, (c + 1) * c_len)

    def head(h):
        return slice(h * hd, (h + 1) * hd)

    def scores(h, c):
        ktc = kt_ref[0, head(h), chunk(c)]
        s_ref[h, c] = _dot(q_ref[0, chunk(c), head(h)], ktc)
        kv_ref[h, c] = _dot((ktc.astype(F32) * kdec_ref[h]).astype(BF16), v_ref[0, chunk(c), head(h)])

    def decay_scores(h, c):
        in_ref[h, c] = (s_ref[h, c] * dec_ref[h]).astype(BF16)

    for i in range(len(items) + 1):
        if i < len(items):
            scores(*items[i])
        if i >= 1:
            decay_scores(*items[i - 1])

    for h in range(q_ref.shape[2] // hd):
        g_chunk = qdec_ref[h, c_len - 1:c_len, :]
        state = jnp.zeros((hd, hd), F32)
        for c in range(n_chunks):
            st_ref[h, c] = state.astype(BF16)
            state = state * g_chunk + kv_ref[h, c]

    def outputs(h, c):
        y_ref[h, chunk(c), :] = (_dot(in_ref[h, c], v_ref[0, chunk(c), head(h)])
                                 + _dot(q_ref[0, chunk(c), head(h)], st_ref[h, c]) * qdec_ref[h])

    def normalise(h, c):
        y = y_ref[h, chunk(c), :]
        mu = jnp.mean(y, axis=-1, keepdims=True)
        yc = y - mu
        var = jnp.mean(yc * yc, axis=-1, keepdims=True)
        o_ref[0, chunk(c), head(h)] = (_silu(g_ref[0, chunk(c), head(h)])
                                       * (yc * lax.rsqrt(var + EPS))).astype(BF16)

    for i in range(len(items) + 1):
        if i < len(items):
            outputs(*items[i])
        if i >= 1:
            normalise(*items[i - 1])


def _retention_tables():
    c_len = RET_CHUNK
    gamma = 1.0 - jnp.power(2.0, -5.0 - jnp.arange(RET_HEADS, dtype=F32))
    log_g = jnp.log(gamma)
    idx = jnp.arange(c_len, dtype=F32)
    rel = idx[:, None] - idx[None, :]
    decay = jnp.where(rel[None] >= 0, jnp.exp(jnp.maximum(rel, 0.0)[None] * log_g[:, None, None]), 0.0)
    k_decay = jnp.exp((c_len - 1 - idx)[None, :] * log_g[:, None])
    q_decay = jnp.exp((idx + 1)[None, :] * log_g[:, None])
    return (decay, jnp.broadcast_to(k_decay[:, None, :], (RET_HEADS, RET_HEAD_DIM, c_len)),
            jnp.broadcast_to(q_decay[:, :, None], (RET_HEADS, c_len, RET_HEAD_DIM)))


def _retention(rq, rkt, rv, rg):
    b, s, _ = rq.shape
    hd, hps, nc = RET_HEAD_DIM, RET_HEADS_PER_STEP, s // RET_CHUNK
    spec = pl.BlockSpec((1, s, hps * hd), lambda i, h: (i, 0, h))
    spec_t = pl.BlockSpec((1, hps * hd, s), lambda i, h: (i, h, 0))
    tables = _retention_tables()
    return pl.pallas_call(
        _retention_kernel,
        grid=(b, RET_HEADS // hps),
        in_specs=[spec, spec_t, spec, spec]
        + [pl.BlockSpec((hps,) + t.shape[1:], lambda i, h: (h, 0, 0)) for t in tables],
        out_specs=spec,
        out_shape=jax.ShapeDtypeStruct((b, s, RET_WIDTH), BF16),
        scratch_shapes=[
            pltpu.VMEM((hps, s, hd), F32),
            pltpu.VMEM((hps, nc, hd, hd), F32),
            pltpu.VMEM((hps, nc, hd, hd), BF16),
            pltpu.VMEM((hps, nc, RET_CHUNK, RET_CHUNK), F32),
            pltpu.VMEM((hps, nc, RET_CHUNK, RET_CHUNK), BF16),
        ],
        compiler_params=_params("arbitrary", "arbitrary"),
        name="retention",
    )(rq, rkt, rv, rg, *tables)


def kernel(x, mem, even_mix_norm, even_w_in, pool_w, pool_scale, sgu_norm, sgu_w, sgu_b, even_w_out,
           odd_mix_norm, odd_w_in, odd_w_out, xattn_norm, mem_norm, xattn_wq, xattn_wkv, xattn_wo,
           ffn_norm, ffn_w_gate_up, ffn_w_down, final_norm):
    depth = xattn_wq.shape[0]
    assert x.shape[1] % (max(d for _, d in DIL_PAIRS) * DIL_BLOCK) == 0 and x.shape[2] == D_MODEL

    def row(v):
        return v.reshape(1, -1)

    kv = _memory_kv(mem, mem_norm, xattn_wkv.astype(BF16))
    h = x
    for layer in range(depth):
        i = layer // 2
        if layer % 2 == 0:
            h = _even_mixer(h, row(even_mix_norm[i]), even_w_in[i].astype(BF16), pool_w[i].astype(BF16),
                            row(pool_scale[i]), row(sgu_norm[i]), sgu_w[i], sgu_b[i].T,
                            even_w_out[i].astype(BF16))
            mix = None
        else:
            cq, ck, cv, rq, rk, rv, rg = _odd_proj(h, row(odd_mix_norm[i]), odd_w_in[i].astype(BF16))
            yc = _dilated_attention(cq, ck, cv)
            yd = _retention(rq, rk, rv, rg)
            mix = (yc, yd, odd_w_out[i].astype(BF16))
        h = _xattn(h, kv, layer, row(xattn_norm[layer]), xattn_wq[layer].astype(BF16),
                   xattn_wo[layer].astype(BF16), mix)
        h = _ffn(h, row(ffn_norm[layer]), ffn_w_gate_up[layer].astype(BF16), ffn_w_down[layer].astype(BF16),
                 row(final_norm), final_norm=layer == depth - 1)
    return h
```

```python
import functools
import math

import jax
import jax.numpy as jnp
from jax import lax
from jax.experimental import pallas as pl
from jax.experimental.pallas import tpu as pltpu

F32 = jnp.float32
BF16 = jnp.bfloat16

EPS = 1e-6
D_MODEL = 1024
N_MEM = 256

POOL_WINDOWS = (2, 4, 8, 16)
POOL_HALO = 8 * len(POOL_WINDOWS)
assert POOL_WINDOWS == tuple(2 ** (g + 1) for g in range(len(POOL_WINDOWS)))
POOL_WIDTH = 512
GROUP_DIM = 128
SGU_WIDTH = 512
SGU_CHUNK = 128
EVEN_IN = POOL_WIDTH + 2 * SGU_WIDTH

DIL_HEADS = 8
DIL_HEAD_DIM = 64
DIL_WIDTH = 512
DIL_PAIRS = ((128, 1), (512, 4), (2048, 16))
DIL_BLOCK = 128
DIL_SLOTS = 3
MASK_BIG = 2.0 ** 126
ROPE_THETA = 500000.0
ROPE_DIM = 16
RET_HEADS = 4
RET_HEAD_DIM = 128
RET_WIDTH = 512
RET_HEADS_PER_STEP = 2
RET_CHUNK = 256
RET_THETA = 10000.0
ODD_IN = 3 * DIL_WIDTH + 4 * RET_WIDTH

XATTN_HEADS = 4
XATTN_HEAD_DIM = 256
D_FF = 2816

LANES = 128
MXU_DIM = 256
FF_STEP = 6 * MXU_DIM
FF_CHUNKS = tuple((lo, min(lo + FF_STEP, D_FF)) for lo in range(0, D_FF, FF_STEP))
SEQ_TILE = 512
WIDE_TILE = 1024
VMEM_LIMIT = 56 * 1024 * 1024


def _params(*sem):
    return pltpu.CompilerParams(dimension_semantics=sem, vmem_limit_bytes=VMEM_LIMIT)


def _const_spec(shape):
    zeros = (0,) * len(shape)
    return pl.BlockSpec(shape, lambda *_: zeros, pipeline_mode=pl.Buffered(1))


def _rms(x, g):
    ms = jnp.mean(x * x, axis=-1, keepdims=True)
    return x * lax.rsqrt(ms + EPS) * g


def _gelu_tanh(x):
    return 0.5 * x * (1.0 + jnp.tanh(math.sqrt(2.0 / math.pi) * (x + 0.044715 * (x * x * x))))


def _silu(x):
    return x * (1.0 / (1.0 + jnp.exp(-x)))


def _dot(a, b):
    return jnp.dot(a, b, preferred_element_type=F32)


def _dot_nt(a, b):
    return lax.dot_general(a, b, (((1,), (1,)), ((), ())), preferred_element_type=F32)


def _fold_pool_kernel(pw_ref, ps_ref, wo_ref, o_ref):
    for g in range(len(POOL_WINDOWS)):
        rows = slice(g * GROUP_DIM, (g + 1) * GROUP_DIM)
        o_ref[rows, :] = jnp.dot(pw_ref[g] * ps_ref[:, rows], wo_ref[rows, :], precision=lax.Precision.HIGHEST,
                                 preferred_element_type=F32).astype(BF16)


def _fold_pool(pool_w, pool_scale, w_out_pool):
    return pl.pallas_call(
        _fold_pool_kernel,
        out_shape=jax.ShapeDtypeStruct(w_out_pool.shape, BF16),
        compiler_params=pltpu.CompilerParams(vmem_limit_bytes=VMEM_LIMIT),
        name="fold_pool",
    )(pool_w, pool_scale, w_out_pool)


def _even_kernel(x_ref, g_ref, win_ref, wp_ref, sn_ref, sw_ref, sb_ref, wout_ref,
                 o_ref, ext_ref, lva_ref, lvb_ref, u_ref, v_ref, ycat_ref):
    si = pl.program_id(1)
    ts = x_ref.shape[1]
    halo, end = POOL_HALO, POOL_HALO + ts
    half = MXU_DIM

    @pl.when(si == 0)
    def _():
        ext_ref[0:halo, :] = jnp.zeros((halo, POOL_WIDTH), F32)

    hn_rows = []
    for r in range(0, ts, ts // 4):
        hn_rows.append(_rms(x_ref[0, r:r + ts // 4, :], g_ref[...]).astype(BF16))
        ext_ref[halo + r:halo + r + ts // 4, :] = _dot(hn_rows[-1], win_ref[:, :POOL_WIDTH])
    hn = jnp.concatenate(hn_rows, axis=0)

    def in_proj(ref, c0, c1):
        ref[:, c0 - c1:c0 - c1 + half] = _dot(hn, win_ref[:, c0:c0 + half])

    def gelu_inplace(ref, c):
        ref[:, c:c + half] = _gelu_tanh(ref[:, c:c + half])

    bufs = (lva_ref, lvb_ref)

    def level(k):
        src = ext_ref if k == 1 else bufs[k % 2]
        dst, lo, back, c0 = bufs[(k - 1) % 2], 8 * k, 2 ** (k - 1), (k - 1) * GROUP_DIM
        dst[lo:end, c0:] = src[lo:end, c0:] + src[lo - back:end - back, c0:]

    u0, v0 = POOL_WIDTH, POOL_WIDTH + SGU_WIDTH
    in_proj(u_ref, u0, u0)
    level(1)
    level(2)
    in_proj(u_ref, u0 + half, u0)
    level(3)
    level(4)
    assert len(POOL_WINDOWS) == 4

    in_proj(v_ref, v0, v0)
    pos = si * ts + lax.broadcasted_iota(jnp.int32, (ts, 1), 0)
    for g, win in enumerate(POOL_WINDOWS):
        cols = slice(g * GROUP_DIM, (g + 1) * GROUP_DIM)
        wsum = bufs[g % 2][halo:end, cols]
        cnt = jnp.minimum(pos + 1, win).astype(F32)
        ycat_ref[:, cols] = (wsum / cnt - ext_ref[halo:end, cols]).astype(BF16)
    ext_ref[0:halo, :] = ext_ref[ts:end, :]

    in_proj(v_ref, v0 + half, v0)
    gelu_inplace(u_ref, 0)
    gelu_inplace(u_ref, half)

    acc = x_ref[0] + _dot(ycat_ref[:, :POOL_WIDTH], wp_ref[...])
    gelu_inplace(v_ref, 0)
    gelu_inplace(v_ref, half)
    vn = _rms(v_ref[...], sn_ref[...]).astype(BF16)

    nc = ts // SGU_CHUNK
    row = lax.broadcasted_iota(jnp.int32, (SGU_CHUNK, SGU_CHUNK), 0)
    col = lax.broadcasted_iota(jnp.int32, (SGU_CHUNK, SGU_CHUNK), 1)
    groups_per_tile = MXU_DIM // GROUP_DIM
    for g in range(SGU_WIDTH // GROUP_DIM):
        cols = slice(g * GROUP_DIM, (g + 1) * GROUP_DIM)
        ws = jnp.where(row >= col, sw_ref[g], 0.0).astype(BF16)
        rhs = jnp.concatenate([vn[c * SGU_CHUNK:(c + 1) * SGU_CHUNK, cols] for c in range(nc)], axis=1)
        mixed = _dot(ws, rhs) + sb_ref[:, g:g + 1]
        for c in range(nc):
            rows = slice(c * SGU_CHUNK, (c + 1) * SGU_CHUNK)
            yb = u_ref[rows, cols] * mixed[:, c * SGU_CHUNK:(c + 1) * SGU_CHUNK]
            ycat_ref[rows, POOL_WIDTH + g * GROUP_DIM:POOL_WIDTH + (g + 1) * GROUP_DIM] = yb.astype(BF16)
        if (g + 1) % groups_per_tile == 0:
            k0 = POOL_WIDTH + (g + 1 - groups_per_tile) * GROUP_DIM
            acc = acc + _dot(ycat_ref[:, k0:k0 + MXU_DIM], wout_ref[k0:k0 + MXU_DIM, :])
    o_ref[0] = acc


def _even_mixer(h, g, w_in, w_pool, sgu_norm, sgu_w, sgu_b_t, w_out):
    b, s, d = h.shape
    ts = WIDE_TILE
    return pl.pallas_call(
        _even_kernel,
        grid=(b, s // ts),
        in_specs=[
            pl.BlockSpec((1, ts, d), lambda i, j: (i, j, 0)),
            _const_spec((1, d)),
            _const_spec((d, EVEN_IN)),
            _const_spec((POOL_WIDTH, d)),
            _const_spec((1, SGU_WIDTH)),
            _const_spec(sgu_w.shape),
            _const_spec(sgu_b_t.shape),
            _const_spec((POOL_WIDTH + SGU_WIDTH, d)),
        ],
        out_specs=pl.BlockSpec((1, ts, d), lambda i, j: (i, j, 0)),
        out_shape=jax.ShapeDtypeStruct(h.shape, F32),
        scratch_shapes=[pltpu.VMEM((POOL_HALO + ts, POOL_WIDTH), F32)] * 3 + [
            pltpu.VMEM((ts, SGU_WIDTH), F32),
            pltpu.VMEM((ts, SGU_WIDTH), F32),
            pltpu.VMEM((ts, POOL_WIDTH + SGU_WIDTH), BF16),
        ],
        compiler_params=_params("arbitrary", "arbitrary"),
        name="even_mixer",
    )(h, g, w_in, w_pool, sgu_norm, sgu_w, sgu_b_t, w_out)


def _kv_kernel(m_ref, g_ref, w_ref, o_ref):
    mn = _rms(m_ref[...], g_ref[0]).astype(BF16)
    o_ref[0] = _dot(mn, w_ref[0]).astype(BF16)


def _memory_kv(mem, mem_norm, wkv):
    b, m, d = mem.shape
    depth = wkv.shape[0]
    rows = b * m
    tm = math.gcd(rows, 2 * SEQ_TILE)
    out = pl.pallas_call(
        _kv_kernel,
        grid=(depth, rows // tm),
        in_specs=[
            pl.BlockSpec((tm, d), lambda l, i: (i, 0)),
            pl.BlockSpec((1, 1, d), lambda l, i: (l, 0, 0)),
            pl.BlockSpec((1, d, 2 * d), lambda l, i: (l, 0, 0)),
        ],
        out_specs=pl.BlockSpec((1, tm, 2 * d), lambda l, i: (l, i, 0)),
        out_shape=jax.ShapeDtypeStruct((depth, rows, 2 * d), BF16),
        compiler_params=_params("arbitrary", "arbitrary"),
        name="memory_kv",
    )(mem.reshape(rows, d), mem_norm.reshape(depth, 1, d), wkv)
    return out.reshape(depth, b, m, 2 * d)


def _xattn_body(x_ref, kv_ref, g_ref, wq_ref, wo_ref, o_ref, q_ref, ocat_ref, s_ref, p_ref):
    ts = x_ref.shape[0]
    top, bot = slice(0, ts // 2), slice(ts // 2, ts)
    q_scale = (XATTN_HEAD_DIM ** -0.5) * math.log2(math.e)

    def cols(h):
        return slice(h * XATTN_HEAD_DIM, (h + 1) * XATTN_HEAD_DIM)

    def q_proj(rows):
        hn = _rms(x_ref[rows, :], g_ref[...]).astype(BF16)
        q_ref[rows, :] = (_dot(hn, wq_ref[...]) * q_scale).astype(BF16)

    def scores(rows):
        for h in range(XATTN_HEADS):
            s_ref[h] = _dot_nt(q_ref[rows, cols(h)], kv_ref[0, 0, :, cols(h)])

    def softmax():
        for h in range(XATTN_HEADS):
            s = s_ref[h]
            p = jnp.exp2(s - jnp.max(s, axis=-1, keepdims=True))
            p_ref[h] = (p * (1.0 / jnp.sum(p, axis=-1, keepdims=True))).astype(BF16)

    def values(rows):
        for h in range(XATTN_HEADS):
            vh = kv_ref[0, 0, :, D_MODEL + h * XATTN_HEAD_DIM:D_MODEL + (h + 1) * XATTN_HEAD_DIM]
            ocat_ref[rows, cols(h)] = _dot(p_ref[h], vh).astype(BF16)

    def out_proj(rows):
        o_ref[0, rows, :] = x_ref[rows, :] + _dot(ocat_ref[rows, :], wo_ref[...])

    q_proj(top)
    scores(top)
    q_proj(bot)
    softmax()
    values(top)
    scores(bot)
    out_proj(top)
    softmax()
    values(bot)
    out_proj(bot)


def _xattn_kernel(h_ref, kv_ref, g_ref, wq_ref, wo_ref, o_ref, q_ref, ocat_ref, s_ref, p_ref):
    _xattn_body(h_ref.at[0], kv_ref, g_ref, wq_ref, wo_ref, o_ref, q_ref, ocat_ref, s_ref, p_ref)


def _mix_xattn_kernel(h_ref, yc_ref, yd_ref, wmix_ref, kv_ref, g_ref, wq_ref, wo_ref, o_ref,
                      q_ref, ocat_ref, s_ref, p_ref, x_ref):
    ts = h_ref.shape[1]
    for r in range(0, ts, ts // 2):
        rows = slice(r, r + ts // 2)
        x_ref[rows, :] = (h_ref[0, rows, :] + _dot(yc_ref[0, rows, :], wmix_ref[:DIL_WIDTH, :])
                          + _dot(yd_ref[0, rows, :], wmix_ref[DIL_WIDTH:, :]))
    _xattn_body(x_ref, kv_ref, g_ref, wq_ref, wo_ref, o_ref, q_ref, ocat_ref, s_ref, p_ref)


def _xattn(h, kv, layer, g, wq, wo, mix=None):
    b, s, d = h.shape
    ts = WIDE_TILE
    tile = pl.BlockSpec((1, ts, d), lambda i, j: (i, j, 0))
    kv_spec = pl.BlockSpec((1, 1, N_MEM, 2 * d), lambda i, j: (layer, i, 0, 0))
    tail_specs = [kv_spec, _const_spec((1, d)), _const_spec((d, d)), _const_spec((d, d))]
    if mix is None:
        kern, in_specs, args = _xattn_kernel, [tile] + tail_specs, (h, kv, g, wq, wo)
    else:
        yc, yd, w_mix = mix
        half = pl.BlockSpec((1, ts, DIL_WIDTH), lambda i, j: (i, j, 0))
        kern = _mix_xattn_kernel
        in_specs = [tile, half, half, _const_spec((d, d))] + tail_specs
        args = (h, yc, yd, w_mix, kv, g, wq, wo)
    return pl.pallas_call(
        kern,
        grid=(b, s // ts),
        in_specs=in_specs,
        out_specs=tile,
        out_shape=jax.ShapeDtypeStruct(h.shape, F32),
        scratch_shapes=[
            pltpu.VMEM((ts, d), BF16),
            pltpu.VMEM((ts, d), BF16),
            pltpu.VMEM((XATTN_HEADS, ts // 2, N_MEM), F32),
            pltpu.VMEM((XATTN_HEADS, ts // 2, N_MEM), BF16),
        ] + ([] if mix is None else [pltpu.VMEM((ts, d), F32)]),
        compiler_params=_params("arbitrary", "arbitrary"),
        name="xattn" if mix is None else "mix_xattn",
    )(*args)


def _ffn_kernel(x_ref, g_ref, wgu_ref, wd_ref, fg_ref, o_ref, *, final_norm):
    tm = x_ref.shape[0]
    hn_rows, gate_rows = [], []
    for r in range(0, tm, tm // 4):
        hn_rows.append(_rms(x_ref[r:r + tm // 4, :], g_ref[...]).astype(BF16))
        gate_rows.append(_dot(hn_rows[-1], wgu_ref[:, FF_CHUNKS[0][0]:FF_CHUNKS[0][1]]))
    hn = jnp.concatenate(hn_rows, axis=0)
    acc = None
    for c, (lo, hi) in enumerate(FF_CHUNKS):
        gate = jnp.concatenate(gate_rows, axis=0) if c == 0 else _dot(hn, wgu_ref[:, lo:hi])
        up = _dot(hn, wgu_ref[:, D_FF + lo:D_FF + hi])
        act = (_silu(gate) * up).astype(BF16)
        acc = (x_ref[...] if acc is None else acc) + _dot(act, wd_ref[lo:hi, :])
    if final_norm:
        acc = _rms(acc, fg_ref[...])
    o_ref[...] = acc


def _ffn(h, g, wgu, wd, fg, final_norm):
    b, s, d = h.shape
    t = b * s
    tm = WIDE_TILE
    tile = pl.BlockSpec((tm, d), lambda i: (i, 0))
    out = pl.pallas_call(
        functools.partial(_ffn_kernel, final_norm=final_norm),
        grid=(t // tm,),
        in_specs=[tile, _const_spec((1, d)), _const_spec((d, 2 * D_FF)), _const_spec((D_FF, d)),
                  _const_spec((1, d))],
        out_specs=tile,
        out_shape=jax.ShapeDtypeStruct((t, d), F32),
        compiler_params=_params("arbitrary"),
        name="ffn_final" if final_norm else "ffn",
    )(h.reshape(t, d), g, wgu, wd, fg)
    return out.reshape(b, s, d)


def _odd_proj_kernel(x_ref, g_ref, w_ref, wkt_ref, qc_ref, qsa_ref, qsb_ref, dc_ref, dsa_ref, dsb_ref,
                     rc_ref, rs_ref, kct_ref, kst_ref,
                     cq_ref, ck_ref, cv_ref, rq_ref, rkt_ref, rv_ref, rg_ref):
    ts = x_ref.shape[1]
    hn_rows, first_rows = [], []
    for r in range(0, ts, ts // 2):
        hn_rows.append(_rms(x_ref[0, r:r + ts // 2, :], g_ref[...]).astype(BF16))
        first_rows.append(_dot(hn_rows[-1], w_ref[:, :DIL_WIDTH]))
    hn = jnp.concatenate(hn_rows, axis=0)

    def section(i):
        if i == 0:
            return jnp.concatenate(first_rows, axis=0)
        return _dot(hn, w_ref[:, i * DIL_WIDTH:(i + 1) * DIL_WIDTH])

    def lane_blocks(z, fn):
        return jnp.concatenate([fn(z[:, j * LANES:(j + 1) * LANES]) for j in range(z.shape[1] // LANES)], axis=1)

    half = ROPE_DIM // 2

    def dil_rot(c_ref, sa_ref, sb_ref):
        c, sa, sb = c_ref[...], sa_ref[...], sb_ref[...]
        return lambda zb: zb * c + pltpu.roll(zb, half, 1) * sa + pltpu.roll(zb, LANES - half, 1) * sb

    cq_ref[0] = lane_blocks(section(0), dil_rot(qc_ref, qsa_ref, qsb_ref))
    ck_ref[0] = lane_blocks(section(1), dil_rot(dc_ref, dsa_ref, dsb_ref))
    cv_ref[0] = section(2)

    rc, rs = rc_ref[...], rs_ref[...]

    def ret_rot(zb):
        return zb * rc + pltpu.roll(zb, RET_HEAD_DIM // 2, 1) * rs

    rq_ref[0] = lane_blocks(section(3), ret_rot).astype(BF16)

    kt = _dot_nt(wkt_ref[...], hn)
    kct, kst = kct_ref[...], kst_ref[...]
    hd = RET_HEAD_DIM
    for h in range(RET_HEADS):
        zb = kt[h * hd:(h + 1) * hd, :]
        swapped = jnp.concatenate([zb[hd // 2:], zb[:hd // 2]], axis=0)
        rkt_ref[0, h * hd:(h + 1) * hd, :] = (zb * kct + swapped * kst).astype(BF16)

    rv_ref[0] = section(5).astype(BF16)
    rg_ref[0] = section(6)


def _rotary_tables(s):
    pos = jnp.arange(s, dtype=jnp.int32).astype(F32)
    lane = jnp.arange(LANES)
    half = ROPE_DIM // 2
    inv = 1.0 / jnp.power(jnp.float32(ROPE_THETA), jnp.arange(half, dtype=F32) / half)
    ang = pos[:, None] * inv[None, :]
    cos, sin = jnp.cos(ang), jnp.sin(ang)
    hl = lane % DIL_HEAD_DIM
    cos_l, sin_l = cos[:, hl % half], sin[:, hl % half]
    dc = jnp.where(hl[None] < ROPE_DIM, cos_l, 1.0)
    dsa = jnp.where((hl[None] >= half) & (hl[None] < ROPE_DIM), sin_l, 0.0)
    dsb = jnp.where(hl[None] < half, -sin_l, 0.0)
    rhalf = RET_HEAD_DIM // 2
    rinv = 1.0 / jnp.power(jnp.float32(RET_THETA), jnp.arange(rhalf, dtype=F32) / rhalf)
    rang = pos[:, None] * rinv[None, :]
    rcos, rsin = jnp.cos(rang), jnp.sin(rang)
    rc = jnp.concatenate([rcos, rcos], axis=1)
    rs = jnp.concatenate([-rsin, rsin], axis=1)
    qs = (DIL_HEAD_DIM ** -0.5) * math.log2(math.e)
    ks = RET_HEAD_DIM ** -0.5
    return (dc * qs, dsa * qs, dsb * qs, dc, dsa, dsb, rc, rs), ((rc * ks).T, (rs * ks).T)


def _odd_proj(h, g, w_in):
    b, s, d = h.shape
    ts = SEQ_TILE
    tables, tables_t = _rotary_tables(s)
    wkt = w_in[:, 3 * DIL_WIDTH + RET_WIDTH:3 * DIL_WIDTH + 2 * RET_WIDTH].T
    tile = pl.BlockSpec((1, ts, d), lambda j, i: (i, j, 0))
    tab = pl.BlockSpec((ts, LANES), lambda j, i: (j, 0))
    tab_t = pl.BlockSpec((LANES, ts), lambda j, i: (0, j))
    out = pl.BlockSpec((1, ts, DIL_WIDTH), lambda j, i: (i, j, 0))
    out_t = pl.BlockSpec((1, RET_WIDTH, ts), lambda j, i: (i, 0, j))
    f32o = jax.ShapeDtypeStruct((b, s, DIL_WIDTH), F32)
    bf16o = jax.ShapeDtypeStruct((b, s, RET_WIDTH), BF16)
    return pl.pallas_call(
        _odd_proj_kernel,
        grid=(s // ts, b),
        in_specs=[tile, _const_spec((1, d)), _const_spec((d, ODD_IN)), _const_spec((RET_WIDTH, d))]
        + [tab] * len(tables) + [tab_t] * len(tables_t),
        out_specs=[out, out, out, out, out_t, out, out],
        out_shape=[f32o, f32o, f32o, bf16o, jax.ShapeDtypeStruct((b, RET_WIDTH, s), BF16), bf16o, f32o],
        compiler_params=_params("arbitrary", "arbitrary"),
        name="odd_proj",
    )(h, g, w_in, wkt, *tables, *tables_t)


def _dilated_kernel(q_ref, k_ref, v_ref, o_ref, cls_ref, far_ref, mrg_ref, sel_ref, bias_ref, s_ref, p_ref, mcur_ref):
    s_len = q_ref.shape[1]
    blk = DIL_BLOCK
    lane = lax.broadcasted_iota(jnp.int32, (blk, LANES), 1)
    first_head = lane < DIL_HEAD_DIM

    qi = lax.broadcasted_iota(jnp.int32, (2 * blk, blk), 1)
    kr = lax.broadcasted_iota(jnp.int32, (2 * blk, blk), 0)
    sel_ref[...] = jnp.where(kr % blk == qi, 1.0, 0.0).astype(BF16)
    bias_ref[0] = jnp.where(jnp.abs(qi + blk - kr - blk // 2) <= blk // 2, 0.0, -MASK_BIG).astype(BF16)
    bias_ref[1] = jnp.where(kr <= qi, 0.0, -MASK_BIG).astype(BF16)

    (_, d_near), (_, dm), (_, df) = DIL_PAIRS
    assert d_near == 1 and df % dm == 0 and all(w // d == blk for w, d in DIL_PAIRS)
    ratio, cls_len = df // dm, s_len // dm
    for i, ref in enumerate((q_ref, k_ref, v_ref)):
        for r in range(dm):
            cls_ref[i, r * cls_len:(r + 1) * cls_len, :] = ref[0, pl.ds(r, cls_len, stride=dm), :]

    blocks = []
    for r in range(dm):
        for c in range(ratio):
            for n in range(s_len // (df * blk)):
                start = r * cls_len + c + n * ratio * blk
                blocks.append(("far", start, ratio, n > 0, pl.ds(start, blk, stride=ratio)))
    for r in range(dm):
        for n in range(cls_len // blk):
            blocks.append(("mid", r * cls_len + n * blk, 1, n > 0, pl.ds(r + n * dm * blk, blk, stride=dm)))
    for n in range(s_len // blk):
        blocks.append(("near", n * blk, 1, n > 0, pl.ds(n * blk, blk)))

    def rows(t, prev=False):
        _, start, stride, _, _ = blocks[t]
        start -= stride * blk if prev else 0
        return pl.ds(start, blk) if stride == 1 else pl.ds(start, blk, stride=stride)

    def load(i, t, prev=False):
        if blocks[t][0] == "near":
            return (q_ref, k_ref, v_ref)[i][0, rows(t, prev), :]
        return cls_ref[i, rows(t, prev), :]

    def load2(i, t):
        x = load(i, t).astype(BF16)
        return jnp.concatenate([load(i, t, True).astype(BF16), x], axis=0) if blocks[t][3] else x

    def n_keys(t):
        return 2 * blk if blocks[t][3] else blk

    def merged(m1, acc1, den1, m2, acc2, den2):
        m = jnp.maximum(m1, m2)
        w1, w2 = jnp.exp2(m1 - m), jnp.exp2(m2 - m)
        return m, w1 * acc1 + w2 * acc2, w1 * den1 + w2 * den2

    def scores(t):
        q = load(0, t)
        q2 = jnp.concatenate([jnp.where(first_head, q, 0.0), jnp.where(first_head, 0.0, q)], axis=0).astype(BF16)
        bias = bias_ref[0] if blocks[t][3] else bias_ref[1, :blk, :]
        s_ref[t % DIL_SLOTS, :, :n_keys(t)] = _dot_nt(jnp.concatenate([q2, sel_ref[...]], axis=1),
                                                      jnp.concatenate([load2(1, t), bias], axis=1))

    def softmax(t):
        s = s_ref[t % DIL_SLOTS, :, :n_keys(t)]
        m = jnp.max(s, axis=-1, keepdims=True)
        p_ref[t % DIL_SLOTS, :, :n_keys(t)] = jnp.exp2(s - m).astype(BF16)
        mcur_ref[t % DIL_SLOTS] = jnp.where(first_head, m[:blk], m[blk:])

    def values(t):
        vext = jnp.concatenate([load2(2, t), jnp.ones((n_keys(t), LANES), BF16)], axis=1)
        o2 = _dot(p_ref[t % DIL_SLOTS, :, :n_keys(t)], vext)
        acc = jnp.where(first_head, o2[:blk, :LANES], o2[blk:, :LANES])
        den = jnp.where(first_head, o2[:blk, LANES:], o2[blk:, LANES:])
        m = mcur_ref[t % DIL_SLOTS]
        kind, dst = blocks[t][0], blocks[t][4]
        if kind == "far":
            far_ref[0, dst, :], far_ref[1, dst, :], far_ref[2, dst, :] = m, acc, den
        elif kind == "mid":
            src = rows(t)
            m, acc, den = merged(m, acc, den, far_ref[0, src, :], far_ref[1, src, :], far_ref[2, src, :])
            mrg_ref[0, dst, :], mrg_ref[1, dst, :], mrg_ref[2, dst, :] = m, acc, den
        else:
            _, acc, den = merged(m, acc, den, mrg_ref[0, dst, :], mrg_ref[1, dst, :], mrg_ref[2, dst, :])
            o_ref[0, dst, :] = (acc / den).astype(BF16)

    for t in range(len(blocks) + 2):
        if t < len(blocks):
            scores(t)
        if 0 <= t - 1 < len(blocks):
            softmax(t - 1)
        if 0 <= t - 2 < len(blocks):
            values(t - 2)


def _dilated_attention(cq, ck, cv):
    b, s, _ = cq.shape
    spec = pl.BlockSpec((1, s, LANES), lambda i, p: (i, 0, p))
    nbr = len(DIL_PAIRS)
    return pl.pallas_call(
        _dilated_kernel,
        grid=(b, DIL_WIDTH // LANES),
        in_specs=[spec] * 3,
        out_specs=spec,
        out_shape=jax.ShapeDtypeStruct((b, s, DIL_WIDTH), BF16),
        scratch_shapes=[pltpu.VMEM((3, s, LANES), F32)] * 3 + [
            pltpu.VMEM((2 * DIL_BLOCK, LANES), BF16),
            pltpu.VMEM((2, 2 * DIL_BLOCK, LANES), BF16),
            pltpu.VMEM((DIL_SLOTS, 2 * DIL_BLOCK, 2 * DIL_BLOCK), F32),
            pltpu.VMEM((DIL_SLOTS, 2 * DIL_BLOCK, 2 * DIL_BLOCK), BF16),
            pltpu.VMEM((DIL_SLOTS, DIL_BLOCK, LANES), F32),
        ],
        compiler_params=_params("arbitrary", "arbitrary"),
        name="dilated_attention",
    )(cq, ck, cv)


def _retention_kernel(q_ref, kt_ref, v_ref, g_ref, dec_ref, kdec_ref, qdec_ref, o_ref,
                      y_ref, kv_ref, st_ref, s_ref, in_ref):
    s_len = q_ref.shape[1]
    hd = RET_HEAD_DIM
    c_len = RET_CHUNK
    n_chunks = s_len // c_len
    items = [(h, c) for c in range(n_chunks) for h in range(q_ref.shape[2] // hd)]

    def chunk(c):
        return slice(c * c_len, (c + 1) * c_len)

    def head(h):
        return slice(h * hd, (h + 1) * hd)

    def scores(h, c):
        ktc = kt_ref[0, head(h), chunk(c)]
        s_ref[h, c] = _dot(q_ref[0, chunk(c), head(h)], ktc)
        kv_ref[h, c] = _dot((ktc.astype(F32) * kdec_ref[h]).astype(BF16), v_ref[0, chunk(c), head(h)])

    def decay_scores(h, c):
        in_ref[h, c] = (s_ref[h, c] * dec_ref[h]).astype(BF16)

    for i in range(len(items) + 1):
        if i < len(items):
            scores(*items[i])
        if i >= 1:
            decay_scores(*items[i - 1])

    for h in range(q_ref.shape[2] // hd):
        g_chunk = qdec_ref[h, c_len - 1:c_len, :]
        state = jnp.zeros((hd, hd), F32)
        for c in range(n_chunks):
            st_ref[h, c] = state.astype(BF16)
            state = state * g_chunk + kv_ref[h, c]

    def outputs(h, c):
        y_ref[h, chunk(c), :] = (_dot(in_ref[h, c], v_ref[0, chunk(c), head(h)])
                                 + _dot(q_ref[0, chunk(c), head(h)], st_ref[h, c]) * qdec_ref[h])

    def normalise(h, c):
        y = y_ref[h, chunk(c), :]
        mu = jnp.mean(y, axis=-1, keepdims=True)
        yc = y - mu
        var = jnp.mean(yc * yc, axis=-1, keepdims=True)
        o_ref[0, chunk(c), head(h)] = (_silu(g_ref[0, chunk(c), head(h)])
                                       * (yc * lax.rsqrt(var + EPS))).astype(BF16)

    for i in range(len(items) + 1):
        if i < len(items):
            outputs(*items[i])
        if i >= 1:
            normalise(*items[i - 1])


def _retention_tables():
    c_len = RET_CHUNK
    gamma = 1.0 - jnp.power(2.0, -5.0 - jnp.arange(RET_HEADS, dtype=F32))
    log_g = jnp.log(gamma)
    idx = jnp.arange(c_len, dtype=F32)
    rel = idx[:, None] - idx[None, :]
    decay = jnp.where(rel[None] >= 0, jnp.exp(jnp.maximum(rel, 0.0)[None] * log_g[:, None, None]), 0.0)
    k_decay = jnp.exp((c_len - 1 - idx)[None, :] * log_g[:, None])
    q_decay = jnp.exp((idx + 1)[None, :] * log_g[:, None])
    return (decay, jnp.broadcast_to(k_decay[:, None, :], (RET_HEADS, RET_HEAD_DIM, c_len)),
            jnp.broadcast_to(q_decay[:, :, None], (RET_HEADS, c_len, RET_HEAD_DIM)))


def _retention(rq, rkt, rv, rg):
    b, s, _ = rq.shape
    hd, hps, nc = RET_HEAD_DIM, RET_HEADS_PER_STEP, s // RET_CHUNK
    spec = pl.BlockSpec((1, s, hps * hd), lambda i, h: (i, 0, h))
    spec_t = pl.BlockSpec((1, hps * hd, s), lambda i, h: (i, h, 0))
    tables = _retention_tables()
    return pl.pallas_call(
        _retention_kernel,
        grid=(b, RET_HEADS // hps),
        in_specs=[spec, spec_t, spec, spec]
        + [pl.BlockSpec((hps,) + t.shape[1:], lambda i, h: (h, 0, 0)) for t in tables],
        out_specs=spec,
        out_shape=jax.ShapeDtypeStruct((b, s, RET_WIDTH), BF16),
        scratch_shapes=[
            pltpu.VMEM((hps, s, hd), F32),
            pltpu.VMEM((hps, nc, hd, hd), F32),
            pltpu.VMEM((hps, nc, hd, hd), BF16),
            pltpu.VMEM((hps, nc, RET_CHUNK, RET_CHUNK), F32),
            pltpu.VMEM((hps, nc, RET_CHUNK, RET_CHUNK), BF16),
        ],
        compiler_params=_params("arbitrary", "arbitrary"),
        name="retention",
    )(rq, rkt, rv, rg, *tables)


def kernel(x, mem, even_mix_norm, even_w_in, pool_w, pool_scale, sgu_norm, sgu_w, sgu_b, even_w_out,
           odd_mix_norm, odd_w_in, odd_w_out, xattn_norm, mem_norm, xattn_wq, xattn_wkv, xattn_wo,
           ffn_norm, ffn_w_gate_up, ffn_w_down, final_norm):
    depth = xattn_wq.shape[0]
    assert x.shape[1] % (max(d for _, d in DIL_PAIRS) * DIL_BLOCK) == 0 and x.shape[2] == D_MODEL

    def row(v):
        return v.reshape(1, -1)

    kv = _memory_kv(mem, mem_norm, xattn_wkv.astype(BF16))
    h = x
    for layer in range(depth):
        i = layer // 2
        if layer % 2 == 0:
            w_pool = _fold_pool(pool_w[i], row(pool_scale[i]), even_w_out[i][:POOL_WIDTH])
            h = _even_mixer(h, row(even_mix_norm[i]), even_w_in[i].astype(BF16), w_pool,
                            row(sgu_norm[i]), sgu_w[i], sgu_b[i].T, even_w_out[i].astype(BF16))
            mix = None
        else:
            cq, ck, cv, rq, rk, rv, rg = _odd_proj(h, row(odd_mix_norm[i]), odd_w_in[i].astype(BF16))
            yc = _dilated_attention(cq, ck, cv)
            yd = _retention(rq, rk, rv, rg)
            mix = (yc, yd, odd_w_out[i].astype(BF16))
        h = _xattn(h, kv, layer, row(xattn_norm[layer]), xattn_wq[layer].astype(BF16),
                   xattn_wo[layer].astype(BF16), mix)
        h = _ffn(h, row(ffn_norm[layer]), ffn_w_gate_up[layer].astype(BF16), ffn_w_down[layer].astype(BF16),
                 row(final_norm), final_norm=layer == depth - 1)
    return h
```

```python
import functools
import math

import jax
import jax.numpy as jnp
from jax import lax
from jax.experimental import pallas as pl
from jax.experimental.pallas import tpu as pltpu

F32 = jnp.float32
BF16 = jnp.bfloat16

EPS = 1e-6
D_MODEL = 1024
N_MEM = 256

POOL_WINDOWS = (2, 4, 8, 16)
POOL_HALO = 8 * len(POOL_WINDOWS)
assert POOL_WINDOWS == tuple(2 ** (g + 1) for g in range(len(POOL_WINDOWS)))
POOL_WIDTH = 512
GROUP_DIM = 128
SGU_WIDTH = 512
SGU_CHUNK = 128
EVEN_IN = POOL_WIDTH + 2 * SGU_WIDTH

DIL_HEADS = 8
DIL_HEAD_DIM = 64
DIL_WIDTH = 512
DIL_PAIRS = ((128, 1), (512, 4), (2048, 16))
DIL_BLOCK = 128
DIL_SLOTS = 3
MASK_BIG = 2.0 ** 126
ROPE_THETA = 500000.0
ROPE_DIM = 16
RET_HEADS = 4
RET_HEAD_DIM = 128
RET_WIDTH = 512
RET_HEADS_PER_STEP = 2
RET_CHUNK = 256
RET_THETA = 10000.0
ODD_IN = 3 * DIL_WIDTH + 4 * RET_WIDTH

XATTN_HEADS = 4
XATTN_HEAD_DIM = 256
D_FF = 2816

LANES = 128
MXU_DIM = 256
FF_STEP = 6 * MXU_DIM
FF_CHUNKS = tuple((lo, min(lo + FF_STEP, D_FF)) for lo in range(0, D_FF, FF_STEP))
SEQ_TILE = 512
WIDE_TILE = 1024
EVEN_TILE = 1024
VMEM_LIMIT = 56 * 1024 * 1024


def _params(*sem):
    return pltpu.CompilerParams(dimension_semantics=sem, vmem_limit_bytes=VMEM_LIMIT)


def _const_spec(shape):
    zeros = (0,) * len(shape)
    return pl.BlockSpec(shape, lambda *_: zeros, pipeline_mode=pl.Buffered(1))


def _rms(x, g):
    ms = jnp.mean(x * x, axis=-1, keepdims=True)
    return x * lax.rsqrt(ms + EPS) * g


def _gelu_tanh(x):
    return 0.5 * x * (1.0 + jnp.tanh(math.sqrt(2.0 / math.pi) * (x + 0.044715 * (x * x * x))))


def _silu(x):
    return x * (1.0 / (1.0 + jnp.exp(-x)))


def _dot(a, b):
    return jnp.dot(a, b, preferred_element_type=F32)


def _dot_nt(a, b):
    return lax.dot_general(a, b, (((1,), (1,)), ((), ())), preferred_element_type=F32)


def _fold_pool_kernel(pw_ref, ps_ref, wo_ref, o_ref):
    for g in range(len(POOL_WINDOWS)):
        rows = slice(g * GROUP_DIM, (g + 1) * GROUP_DIM)
        o_ref[rows, :] = jnp.dot(pw_ref[g] * ps_ref[:, rows], wo_ref[rows, :], precision=lax.Precision.HIGHEST,
                                 preferred_element_type=F32).astype(BF16)


def _fold_pool(pool_w, pool_scale, w_out_pool):
    return pl.pallas_call(
        _fold_pool_kernel,
        out_shape=jax.ShapeDtypeStruct(w_out_pool.shape, BF16),
        compiler_params=pltpu.CompilerParams(vmem_limit_bytes=VMEM_LIMIT),
        name="fold_pool",
    )(pool_w, pool_scale, w_out_pool)


def _even_body(x_ref, g_ref, win_ref, wp_ref, sn_ref, sw_ref, sb_ref, wout_ref,
               o_ref, ext_ref, lva_ref, lvb_ref, u_ref, v_ref, ycat_ref):
    si = pl.program_id(1)
    ts = x_ref.shape[0]
    halo, end = POOL_HALO, POOL_HALO + ts
    half = MXU_DIM

    @pl.when(si == 0)
    def _():
        ext_ref[0:halo, :] = jnp.zeros((halo, POOL_WIDTH), F32)

    hn_rows = []
    for r in range(0, ts, ts // 4):
        hn_rows.append(_rms(x_ref[r:r + ts // 4, :], g_ref[...]).astype(BF16))
        ext_ref[halo + r:halo + r + ts // 4, :] = _dot(hn_rows[-1], win_ref[:, :POOL_WIDTH])
    hn = jnp.concatenate(hn_rows, axis=0)

    def in_proj(ref, c0, c1):
        ref[:, c0 - c1:c0 - c1 + half] = _dot(hn, win_ref[:, c0:c0 + half])

    def gelu_inplace(ref, c):
        ref[:, c:c + half] = _gelu_tanh(ref[:, c:c + half])

    bufs = (lva_ref, lvb_ref)

    def level(k):
        src = ext_ref if k == 1 else bufs[k % 2]
        dst, lo, back, c0 = bufs[(k - 1) % 2], 8 * k, 2 ** (k - 1), (k - 1) * GROUP_DIM
        dst[lo:end, c0:] = src[lo:end, c0:] + src[lo - back:end - back, c0:]

    u0, v0 = POOL_WIDTH, POOL_WIDTH + SGU_WIDTH
    in_proj(u_ref, u0, u0)
    level(1)
    level(2)
    in_proj(u_ref, u0 + half, u0)
    level(3)
    level(4)
    assert len(POOL_WINDOWS) == 4

    in_proj(v_ref, v0, v0)
    pos = si * ts + lax.broadcasted_iota(jnp.int32, (ts, 1), 0)
    for g, win in enumerate(POOL_WINDOWS):
        cols = slice(g * GROUP_DIM, (g + 1) * GROUP_DIM)
        wsum = bufs[g % 2][halo:end, cols]
        cnt = jnp.minimum(pos + 1, win).astype(F32)
        ycat_ref[:, cols] = (wsum / cnt - ext_ref[halo:end, cols]).astype(BF16)
    ext_ref[0:halo, :] = ext_ref[ts:end, :]

    in_proj(v_ref, v0 + half, v0)
    gelu_inplace(u_ref, 0)
    gelu_inplace(u_ref, half)

    acc = x_ref[...] + _dot(ycat_ref[:, :POOL_WIDTH], wp_ref[...])
    gelu_inplace(v_ref, 0)
    gelu_inplace(v_ref, half)
    vn = _rms(v_ref[...], sn_ref[...]).astype(BF16)

    nc = ts // SGU_CHUNK
    row = lax.broadcasted_iota(jnp.int32, (SGU_CHUNK, SGU_CHUNK), 0)
    col = lax.broadcasted_iota(jnp.int32, (SGU_CHUNK, SGU_CHUNK), 1)
    groups_per_tile = MXU_DIM // GROUP_DIM
    for g in range(SGU_WIDTH // GROUP_DIM):
        cols = slice(g * GROUP_DIM, (g + 1) * GROUP_DIM)
        ws = jnp.where(row >= col, sw_ref[g], 0.0).astype(BF16)
        rhs = jnp.concatenate([vn[c * SGU_CHUNK:(c + 1) * SGU_CHUNK, cols] for c in range(nc)], axis=1)
        mixed = _dot(ws, rhs) + sb_ref[:, g:g + 1]
        for c in range(nc):
            rows = slice(c * SGU_CHUNK, (c + 1) * SGU_CHUNK)
            yb = u_ref[rows, cols] * mixed[:, c * SGU_CHUNK:(c + 1) * SGU_CHUNK]
            ycat_ref[rows, POOL_WIDTH + g * GROUP_DIM:POOL_WIDTH + (g + 1) * GROUP_DIM] = yb.astype(BF16)
        if (g + 1) % groups_per_tile == 0:
            k0 = POOL_WIDTH + (g + 1 - groups_per_tile) * GROUP_DIM
            acc = acc + _dot(ycat_ref[:, k0:k0 + MXU_DIM], wout_ref[k0:k0 + MXU_DIM, :])
    o_ref[...] = acc


def _even_xattn_kernel(x_ref, g_ref, win_ref, wp_ref, sn_ref, sw_ref, sb_ref, wout_ref,
                       kv_ref, xg_ref, wq_ref, wo_ref, o_ref,
                       ext_ref, lva_ref, lvb_ref, u_ref, v_ref, ycat_ref, h_ref, q_ref, ocat_ref, s_ref, p_ref):
    _even_body(x_ref.at[0], g_ref, win_ref, wp_ref, sn_ref, sw_ref, sb_ref, wout_ref,
               h_ref, ext_ref, lva_ref, lvb_ref, u_ref, v_ref, ycat_ref)
    _xattn_body(h_ref, kv_ref, xg_ref, wq_ref, wo_ref, o_ref, q_ref, ocat_ref, s_ref, p_ref)


def _even_mixer_xattn(h, g, w_in, w_pool, sgu_norm, sgu_w, sgu_b_t, w_out, kv, layer, xg, wq, wo):
    b, s, d = h.shape
    ts = EVEN_TILE
    tile = pl.BlockSpec((1, ts, d), lambda i, j: (i, j, 0))
    return pl.pallas_call(
        _even_xattn_kernel,
        grid=(b, s // ts),
        in_specs=[
            tile,
            _const_spec((1, d)),
            _const_spec((d, EVEN_IN)),
            _const_spec((POOL_WIDTH, d)),
            _const_spec((1, SGU_WIDTH)),
            _const_spec(sgu_w.shape),
            _const_spec(sgu_b_t.shape),
            _const_spec((POOL_WIDTH + SGU_WIDTH, d)),
            pl.BlockSpec((1, 1, N_MEM, 2 * d), lambda i, j: (layer, i, 0, 0)),
            _const_spec((1, d)),
            _const_spec((d, d)),
            _const_spec((d, d)),
        ],
        out_specs=tile,
        out_shape=jax.ShapeDtypeStruct(h.shape, F32),
        scratch_shapes=[pltpu.VMEM((POOL_HALO + ts, POOL_WIDTH), F32)] * 3 + [
            pltpu.VMEM((ts, SGU_WIDTH), F32),
            pltpu.VMEM((ts, SGU_WIDTH), F32),
            pltpu.VMEM((ts, POOL_WIDTH + SGU_WIDTH), BF16),
            pltpu.VMEM((ts, d), F32),
            pltpu.VMEM((ts, d), BF16),
            pltpu.VMEM((ts, d), BF16),
            pltpu.VMEM((XATTN_HEADS, ts // 2, N_MEM), F32),
            pltpu.VMEM((XATTN_HEADS, ts // 2, N_MEM), BF16),
        ],
        compiler_params=_params("arbitrary", "arbitrary"),
        name="even_mixer_xattn",
    )(h, g, w_in, w_pool, sgu_norm, sgu_w, sgu_b_t, w_out, kv, xg, wq, wo)


def _kv_kernel(m_ref, g_ref, w_ref, o_ref):
    mn = _rms(m_ref[...], g_ref[0]).astype(BF16)
    o_ref[0] = _dot(mn, w_ref[0]).astype(BF16)


def _memory_kv(mem, mem_norm, wkv):
    b, m, d = mem.shape
    depth = wkv.shape[0]
    rows = b * m
    tm = math.gcd(rows, 2 * SEQ_TILE)
    out = pl.pallas_call(
        _kv_kernel,
        grid=(depth, rows // tm),
        in_specs=[
            pl.BlockSpec((tm, d), lambda l, i: (i, 0)),
            pl.BlockSpec((1, 1, d), lambda l, i: (l, 0, 0)),
            pl.BlockSpec((1, d, 2 * d), lambda l, i: (l, 0, 0)),
        ],
        out_specs=pl.BlockSpec((1, tm, 2 * d), lambda l, i: (l, i, 0)),
        out_shape=jax.ShapeDtypeStruct((depth, rows, 2 * d), BF16),
        compiler_params=_params("arbitrary", "arbitrary"),
        name="memory_kv",
    )(mem.reshape(rows, d), mem_norm.reshape(depth, 1, d), wkv)
    return out.reshape(depth, b, m, 2 * d)


def _xattn_body(x_ref, kv_ref, g_ref, wq_ref, wo_ref, o_ref, q_ref, ocat_ref, s_ref, p_ref):
    ts = x_ref.shape[0]
    top, bot = slice(0, ts // 2), slice(ts // 2, ts)
    q_scale = (XATTN_HEAD_DIM ** -0.5) * math.log2(math.e)

    def cols(h):
        return slice(h * XATTN_HEAD_DIM, (h + 1) * XATTN_HEAD_DIM)

    def q_proj(rows):
        hn = _rms(x_ref[rows, :], g_ref[...]).astype(BF16)
        q_ref[rows, :] = (_dot(hn, wq_ref[...]) * q_scale).astype(BF16)

    def scores(rows):
        for h in range(XATTN_HEADS):
            s_ref[h] = _dot_nt(q_ref[rows, cols(h)], kv_ref[0, 0, :, cols(h)])

    def softmax():
        for h in range(XATTN_HEADS):
            s = s_ref[h]
            p = jnp.exp2(s - jnp.max(s, axis=-1, keepdims=True))
            p_ref[h] = (p * (1.0 / jnp.sum(p, axis=-1, keepdims=True))).astype(BF16)

    def values(rows):
        for h in range(XATTN_HEADS):
            vh = kv_ref[0, 0, :, D_MODEL + h * XATTN_HEAD_DIM:D_MODEL + (h + 1) * XATTN_HEAD_DIM]
            ocat_ref[rows, cols(h)] = _dot(p_ref[h], vh).astype(BF16)

    def out_proj(rows):
        o_ref[0, rows, :] = x_ref[rows, :] + _dot(ocat_ref[rows, :], wo_ref[...])

    q_proj(top)
    scores(top)
    q_proj(bot)
    softmax()
    values(top)
    scores(bot)
    out_proj(top)
    softmax()
    values(bot)
    out_proj(bot)


def _xattn_kernel(h_ref, kv_ref, g_ref, wq_ref, wo_ref, o_ref, q_ref, ocat_ref, s_ref, p_ref):
    _xattn_body(h_ref.at[0], kv_ref, g_ref, wq_ref, wo_ref, o_ref, q_ref, ocat_ref, s_ref, p_ref)


def _mix_xattn_kernel(h_ref, yc_ref, yd_ref, wmix_ref, kv_ref, g_ref, wq_ref, wo_ref, o_ref,
                      q_ref, ocat_ref, s_ref, p_ref, x_ref):
    ts = h_ref.shape[1]
    for r in range(0, ts, ts // 2):
        rows = slice(r, r + ts // 2)
        x_ref[rows, :] = (h_ref[0, rows, :] + _dot(yc_ref[0, rows, :], wmix_ref[:DIL_WIDTH, :])
                          + _dot(yd_ref[0, rows, :], wmix_ref[DIL_WIDTH:, :]))
    _xattn_body(x_ref, kv_ref, g_ref, wq_ref, wo_ref, o_ref, q_ref, ocat_ref, s_ref, p_ref)


def _xattn(h, kv, layer, g, wq, wo, mix=None):
    b, s, d = h.shape
    ts = WIDE_TILE
    tile = pl.BlockSpec((1, ts, d), lambda i, j: (i, j, 0))
    kv_spec = pl.BlockSpec((1, 1, N_MEM, 2 * d), lambda i, j: (layer, i, 0, 0))
    tail_specs = [kv_spec, _const_spec((1, d)), _const_spec((d, d)), _const_spec((d, d))]
    if mix is None:
        kern, in_specs, args = _xattn_kernel, [tile] + tail_specs, (h, kv, g, wq, wo)
    else:
        yc, yd, w_mix = mix
        half = pl.BlockSpec((1, ts, DIL_WIDTH), lambda i, j: (i, j, 0))
        kern = _mix_xattn_kernel
        in_specs = [tile, half, half, _const_spec((d, d))] + tail_specs
        args = (h, yc, yd, w_mix, kv, g, wq, wo)
    return pl.pallas_call(
        kern,
        grid=(b, s // ts),
        in_specs=in_specs,
        out_specs=tile,
        out_shape=jax.ShapeDtypeStruct(h.shape, F32),
        scratch_shapes=[
            pltpu.VMEM((ts, d), BF16),
            pltpu.VMEM((ts, d), BF16),
            pltpu.VMEM((XATTN_HEADS, ts // 2, N_MEM), F32),
            pltpu.VMEM((XATTN_HEADS, ts // 2, N_MEM), BF16),
        ] + ([] if mix is None else [pltpu.VMEM((ts, d), F32)]),
        compiler_params=_params("arbitrary", "arbitrary"),
        name="xattn" if mix is None else "mix_xattn",
    )(*args)


def _ffn_kernel(x_ref, g_ref, wgu_ref, wd_ref, fg_ref, o_ref, *, final_norm):
    tm = x_ref.shape[0]
    hn_rows, gate_rows = [], []
    for r in range(0, tm, tm // 4):
        hn_rows.append(_rms(x_ref[r:r + tm // 4, :], g_ref[...]).astype(BF16))
        gate_rows.append(_dot(hn_rows[-1], wgu_ref[:, FF_CHUNKS[0][0]:FF_CHUNKS[0][1]]))
    hn = jnp.concatenate(hn_rows, axis=0)
    acc = None
    for c, (lo, hi) in enumerate(FF_CHUNKS):
        gate = jnp.concatenate(gate_rows, axis=0) if c == 0 else _dot(hn, wgu_ref[:, lo:hi])
        up = _dot(hn, wgu_ref[:, D_FF + lo:D_FF + hi])
        act = (_silu(gate) * up).astype(BF16)
        acc = (x_ref[...] if acc is None else acc) + _dot(act, wd_ref[lo:hi, :])
    if final_norm:
        acc = _rms(acc, fg_ref[...])
    o_ref[...] = acc


def _ffn(h, g, wgu, wd, fg, final_norm):
    b, s, d = h.shape
    t = b * s
    tm = WIDE_TILE
    tile = pl.BlockSpec((tm, d), lambda i: (i, 0))
    out = pl.pallas_call(
        functools.partial(_ffn_kernel, final_norm=final_norm),
        grid=(t // tm,),
        in_specs=[tile, _const_spec((1, d)), _const_spec((d, 2 * D_FF)), _const_spec((D_FF, d)),
                  _const_spec((1, d))],
        out_specs=tile,
        out_shape=jax.ShapeDtypeStruct((t, d), F32),
        compiler_params=_params("arbitrary"),
        name="ffn_final" if final_norm else "ffn",
    )(h.reshape(t, d), g, wgu, wd, fg)
    return out.reshape(b, s, d)


def _odd_proj_kernel(x_ref, g_ref, w_ref, wkt_ref, qc_ref, qsa_ref, qsb_ref, dc_ref, dsa_ref, dsb_ref,
                     rc_ref, rs_ref, kct_ref, kst_ref,
                     cq_ref, ck_ref, cv_ref, rq_ref, rkt_ref, rv_ref, rg_ref):
    ts = x_ref.shape[1]
    hn_rows, first_rows = [], []
    for r in range(0, ts, ts // 2):
        hn_rows.append(_rms(x_ref[0, r:r + ts // 2, :], g_ref[...]).astype(BF16))
        first_rows.append(_dot(hn_rows[-1], w_ref[:, :DIL_WIDTH]))
    hn = jnp.concatenate(hn_rows, axis=0)

    def section(i):
        if i == 0:
            return jnp.concatenate(first_rows, axis=0)
        return _dot(hn, w_ref[:, i * DIL_WIDTH:(i + 1) * DIL_WIDTH])

    def lane_blocks(z, fn):
        return jnp.concatenate([fn(z[:, j * LANES:(j + 1) * LANES]) for j in range(z.shape[1] // LANES)], axis=1)

    half = ROPE_DIM // 2

    def dil_rot(c_ref, sa_ref, sb_ref):
        c, sa, sb = c_ref[...], sa_ref[...], sb_ref[...]
        return lambda zb: zb * c + pltpu.roll(zb, half, 1) * sa + pltpu.roll(zb, LANES - half, 1) * sb

    cq_ref[0] = lane_blocks(section(0), dil_rot(qc_ref, qsa_ref, qsb_ref))
    ck_ref[0] = lane_blocks(section(1), dil_rot(dc_ref, dsa_ref, dsb_ref))
    cv_ref[0] = section(2)

    rc, rs = rc_ref[...], rs_ref[...]

    def ret_rot(zb):
        return zb * rc + pltpu.roll(zb, RET_HEAD_DIM // 2, 1) * rs

    rq_ref[0] = lane_blocks(section(3), ret_rot).astype(BF16)

    kt = _dot_nt(wkt_ref[...], hn)
    kct, kst = kct_ref[...], kst_ref[...]
    hd = RET_HEAD_DIM
    for h in range(RET_HEADS):
        zb = kt[h * hd:(h + 1) * hd, :]
        swapped = jnp.concatenate([zb[hd // 2:], zb[:hd // 2]], axis=0)
        rkt_ref[0, h * hd:(h + 1) * hd, :] = (zb * kct + swapped * kst).astype(BF16)

    rv_ref[0] = section(5).astype(BF16)
    rg_ref[0] = section(6)


def _rotary_tables(s):
    pos = jnp.arange(s, dtype=jnp.int32).astype(F32)
    lane = jnp.arange(LANES)
    half = ROPE_DIM // 2
    inv = 1.0 / jnp.power(jnp.float32(ROPE_THETA), jnp.arange(half, dtype=F32) / half)
    ang = pos[:, None] * inv[None, :]
    cos, sin = jnp.cos(ang), jnp.sin(ang)
    hl = lane % DIL_HEAD_DIM
    cos_l, sin_l = cos[:, hl % half], sin[:, hl % half]
    dc = jnp.where(hl[None] < ROPE_DIM, cos_l, 1.0)
    dsa = jnp.where((hl[None] >= half) & (hl[None] < ROPE_DIM), sin_l, 0.0)
    dsb = jnp.where(hl[None] < half, -sin_l, 0.0)
    rhalf = RET_HEAD_DIM // 2
    rinv = 1.0 / jnp.power(jnp.float32(RET_THETA), jnp.arange(rhalf, dtype=F32) / rhalf)
    rang = pos[:, None] * rinv[None, :]
    rcos, rsin = jnp.cos(rang), jnp.sin(rang)
    rc = jnp.concatenate([rcos, rcos], axis=1)
    rs = jnp.concatenate([-rsin, rsin], axis=1)
    qs = (DIL_HEAD_DIM ** -0.5) * math.log2(math.e)
    ks = RET_HEAD_DIM ** -0.5
    return (dc * qs, dsa * qs, dsb * qs, dc, dsa, dsb, rc, rs), ((rc * ks).T, (rs * ks).T)


def _odd_proj(h, g, w_in):
    b, s, d = h.shape
    ts = SEQ_TILE
    tables, tables_t = _rotary_tables(s)
    wkt = w_in[:, 3 * DIL_WIDTH + RET_WIDTH:3 * DIL_WIDTH + 2 * RET_WIDTH].T
    tile = pl.BlockSpec((1, ts, d), lambda j, i: (i, j, 0))
    tab = pl.BlockSpec((ts, LANES), lambda j, i: (j, 0))
    tab_t = pl.BlockSpec((LANES, ts), lambda j, i: (0, j))
    out = pl.BlockSpec((1, ts, DIL_WIDTH), lambda j, i: (i, j, 0))
    out_t = pl.BlockSpec((1, RET_WIDTH, ts), lambda j, i: (i, 0, j))
    f32o = jax.ShapeDtypeStruct((b, s, DIL_WIDTH), F32)
    bf16o = jax.ShapeDtypeStruct((b, s, RET_WIDTH), BF16)
    return pl.pallas_call(
        _odd_proj_kernel,
        grid=(s // ts, b),
        in_specs=[tile, _const_spec((1, d)), _const_spec((d, ODD_IN)), _const_spec((RET_WIDTH, d))]
        + [tab] * len(tables) + [tab_t] * len(tables_t),
        out_specs=[out, out, out, out, out_t, out, out],
        out_shape=[f32o, f32o, f32o, bf16o, jax.ShapeDtypeStruct((b, RET_WIDTH, s), BF16), bf16o, f32o],
        compiler_params=_params("arbitrary", "arbitrary"),
        name="odd_proj",
    )(h, g, w_in, wkt, *tables, *tables_t)


def _dilated_kernel(q_ref, k_ref, v_ref, o_ref, cls_ref, far_ref, mrg_ref, sel_ref, bias_ref, s_ref, p_ref, mcur_ref):
    s_len = q_ref.shape[1]
    blk = DIL_BLOCK
    lane = lax.broadcasted_iota(jnp.int32, (blk, LANES), 1)
    first_head = lane < DIL_HEAD_DIM

    qi = lax.broadcasted_iota(jnp.int32, (2 * blk, blk), 1)
    kr = lax.broadcasted_iota(jnp.int32, (2 * blk, blk), 0)
    sel_ref[...] = jnp.where(kr % blk == qi, 1.0, 0.0).astype(BF16)
    bias_ref[0] = jnp.where(jnp.abs(qi + blk - kr - blk // 2) <= blk // 2, 0.0, -MASK_BIG).astype(BF16)
    bias_ref[1] = jnp.where(kr <= qi, 0.0, -MASK_BIG).astype(BF16)

    (_, d_near), (_, dm), (_, df) = DIL_PAIRS
    assert d_near == 1 and df % dm == 0 and all(w // d == blk for w, d in DIL_PAIRS)
    ratio, cls_len = df // dm, s_len // dm
    for i, ref in enumerate((q_ref, k_ref, v_ref)):
        for r in range(dm):
            cls_ref[i, r * cls_len:(r + 1) * cls_len, :] = ref[0, pl.ds(r, cls_len, stride=dm), :]

    blocks = []
    for r in range(dm):
        for c in range(ratio):
            for n in range(s_len // (df * blk)):
                start = r * cls_len + c + n * ratio * blk
                blocks.append(("far", start, ratio, n > 0, pl.ds(start, blk, stride=ratio)))
    for r in range(dm):
        for n in range(cls_len // blk):
            blocks.append(("mid", r * cls_len + n * blk, 1, n > 0, pl.ds(r + n * dm * blk, blk, stride=dm)))
    for n in range(s_len // blk):
        blocks.append(("near", n * blk, 1, n > 0, pl.ds(n * blk, blk)))

    def rows(t, prev=False):
        _, start, stride, _, _ = blocks[t]
        start -= stride * blk if prev else 0
        return pl.ds(start, blk) if stride == 1 else pl.ds(start, blk, stride=stride)

    def load(i, t, prev=False):
        if blocks[t][0] == "near":
            return (q_ref, k_ref, v_ref)[i][0, rows(t, prev), :]
        return cls_ref[i, rows(t, prev), :]

    def load2(i, t):
        x = load(i, t).astype(BF16)
        return jnp.concatenate([load(i, t, True).astype(BF16), x], axis=0) if blocks[t][3] else x

    def n_keys(t):
        return 2 * blk if blocks[t][3] else blk

    def merged(m1, acc1, den1, m2, acc2, den2):
        m = jnp.maximum(m1, m2)
        w1, w2 = jnp.exp2(m1 - m), jnp.exp2(m2 - m)
        return m, w1 * acc1 + w2 * acc2, w1 * den1 + w2 * den2

    def scores(t):
        q = load(0, t)
        q2 = jnp.concatenate([jnp.where(first_head, q, 0.0), jnp.where(first_head, 0.0, q)], axis=0).astype(BF16)
        bias = bias_ref[0] if blocks[t][3] else bias_ref[1, :blk, :]
        s_ref[t % DIL_SLOTS, :, :n_keys(t)] = _dot_nt(jnp.concatenate([q2, sel_ref[...]], axis=1),
                                                      jnp.concatenate([load2(1, t), bias], axis=1))

    def softmax(t):
        s = s_ref[t % DIL_SLOTS, :, :n_keys(t)]
        m = jnp.max(s, axis=-1, keepdims=True)
        p_ref[t % DIL_SLOTS, :, :n_keys(t)] = jnp.exp2(s - m).astype(BF16)
        mcur_ref[t % DIL_SLOTS] = jnp.where(first_head, m[:blk], m[blk:])

    def values(t):
        vext = jnp.concatenate([load2(2, t), jnp.ones((n_keys(t), LANES), BF16)], axis=1)
        o2 = _dot(p_ref[t % DIL_SLOTS, :, :n_keys(t)], vext)
        acc = jnp.where(first_head, o2[:blk, :LANES], o2[blk:, :LANES])
        den = jnp.where(first_head, o2[:blk, LANES:], o2[blk:, LANES:])
        m = mcur_ref[t % DIL_SLOTS]
        kind, dst = blocks[t][0], blocks[t][4]
        if kind == "far":
            far_ref[0, dst, :], far_ref[1, dst, :], far_ref[2, dst, :] = m, acc, den
        elif kind == "mid":
            src = rows(t)
            m, acc, den = merged(m, acc, den, far_ref[0, src, :], far_ref[1, src, :], far_ref[2, src, :])
            mrg_ref[0, dst, :], mrg_ref[1, dst, :], mrg_ref[2, dst, :] = m, acc, den
        else:
            _, acc, den = merged(m, acc, den, mrg_ref[0, dst, :], mrg_ref[1, dst, :], mrg_ref[2, dst, :])
            o_ref[0, dst, :] = (acc / den).astype(BF16)

    for t in range(len(blocks) + 2):
        if t < len(blocks):
            scores(t)
        if 0 <= t - 1 < len(blocks):
            softmax(t - 1)
        if 0 <= t - 2 < len(blocks):
            values(t - 2)


def _dilated_attention(cq, ck, cv):
    b, s, _ = cq.shape
    spec = pl.BlockSpec((1, s, LANES), lambda i, p: (i, 0, p))
    nbr = len(DIL_PAIRS)
    return pl.pallas_call(
        _dilated_kernel,
        grid=(b, DIL_WIDTH // LANES),
        in_specs=[spec] * 3,
        out_specs=spec,
        out_shape=jax.ShapeDtypeStruct((b, s, DIL_WIDTH), BF16),
        scratch_shapes=[pltpu.VMEM((3, s, LANES), F32)] * 3 + [
            pltpu.VMEM((2 * DIL_BLOCK, LANES), BF16),
            pltpu.VMEM((2, 2 * DIL_BLOCK, LANES), BF16),
            pltpu.VMEM((DIL_SLOTS, 2 * DIL_BLOCK, 2 * DIL_BLOCK), F32),
            pltpu.VMEM((DIL_SLOTS, 2 * DIL_BLOCK, 2 * DIL_BLOCK), BF16),
            pltpu.VMEM((DIL_SLOTS, DIL_BLOCK, LANES), F32),
        ],
        compiler_params=_params("arbitrary", "arbitrary"),
        name="dilated_attention",
    )(cq, ck, cv)


def _retention_kernel(q_ref, kt_ref, v_ref, g_ref, dec_ref, kdec_ref, qdec_ref, o_ref,
                      y_ref, kv_ref, st_ref, s_ref, in_ref):
    s_len = q_ref.shape[1]
    hd = RET_HEAD_DIM
    c_len = RET_CHUNK
    n_chunks = s_len // c_len
    items = [(h, c) for c in range(n_chunks) for h in range(q_ref.shape[2] // hd)]

    def chunk(c):
        return slice(c * c_len, (c + 1) * c_len)

    def head(h):
        return slice(h * hd, (h + 1) * hd)

    def scores(h, c):
        ktc = kt_ref[0, head(h), chunk(c)]
        s_ref[h, c] = _dot(q_ref[0, chunk(c), head(h)], ktc)
        kv_ref[h, c] = _dot((ktc.astype(F32) * kdec_ref[h]).astype(BF16), v_ref[0, chunk(c), head(h)])

    def decay_scores(h, c):
        in_ref[h, c] = (s_ref[h, c] * dec_ref[h]).astype(BF16)

    for i in range(len(items) + 1):
        if i < len(items):
            scores(*items[i])
        if i >= 1:
            decay_scores(*items[i - 1])

    for h in range(q_ref.shape[2] // hd):
        g_chunk = qdec_ref[h, c_len - 1:c_len, :]
        state = jnp.zeros((hd, hd), F32)
        for c in range(n_chunks):
            st_ref[h, c] = state.astype(BF16)
            state = state * g_chunk + kv_ref[h, c]

    def outputs(h, c):
        y_ref[h, chunk(c), :] = (_dot(in_ref[h, c], v_ref[0, chunk(c), head(h)])
                                 + _dot(q_ref[0, chunk(c), head(h)], st_ref[h, c]) * qdec_ref[h])

    def normalise(h, c):
        y = y_ref[h, chunk(c), :]
        mu = jnp.mean(y, axis=-1, keepdims=True)
        yc = y - mu
        var = jnp.mean(yc * yc, axis=-1, keepdims=True)
        o_ref[0, chunk(c), head(h)] = (_silu(g_ref[0, chunk(c), head(h)])
                                       * (yc * lax.rsqrt(var + EPS))).astype(BF16)

    for i in range(len(items) + 1):
        if i < len(items):
            outputs(*items[i])
        if i >= 1:
            normalise(*items[i - 1])


def _retention_tables():
    c_len = RET_CHUNK
    gamma = 1.0 - jnp.power(2.0, -5.0 - jnp.arange(RET_HEADS, dtype=F32))
    log_g = jnp.log(gamma)
    idx = jnp.arange(c_len, dtype=F32)
    rel = idx[:, None] - idx[None, :]
    decay = jnp.where(rel[None] >= 0, jnp.exp(jnp.maximum(rel, 0.0)[None] * log_g[:, None, None]), 0.0)
    k_decay = jnp.exp((c_len - 1 - idx)[None, :] * log_g[:, None])
    q_decay = jnp.exp((idx + 1)[None, :] * log_g[:, None])
    return (decay, jnp.broadcast_to(k_decay[:, None, :], (RET_HEADS, RET_HEAD_DIM, c_len)),
            jnp.broadcast_to(q_decay[:, :, None], (RET_HEADS, c_len, RET_HEAD_DIM)))


def _retention(rq, rkt, rv, rg):
    b, s, _ = rq.shape
    hd, hps, nc = RET_HEAD_DIM, RET_HEADS_PER_STEP, s // RET_CHUNK
    spec = pl.BlockSpec((1, s, hps * hd), lambda i, h: (i, 0, h))
    spec_t = pl.BlockSpec((1, hps * hd, s), lambda i, h: (i, h, 0))
    tables = _retention_tables()
    return pl.pallas_call(
        _retention_kernel,
        grid=(b, RET_HEADS // hps),
        in_specs=[spec, spec_t, spec, spec]
        + [pl.BlockSpec((hps,) + t.shape[1:], lambda i, h: (h, 0, 0)) for t in tables],
        out_specs=spec,
        out_shape=jax.ShapeDtypeStruct((b, s, RET_WIDTH), BF16),
        scratch_shapes=[
            pltpu.VMEM((hps, s, hd), F32),
            pltpu.VMEM((hps, nc, hd, hd), F32),
            pltpu.VMEM((hps, nc, hd, hd), BF16),
            pltpu.VMEM((hps, nc, RET_CHUNK, RET_CHUNK), F32),
            pltpu.VMEM((hps, nc, RET_CHUNK, RET_CHUNK), BF16),
        ],
        compiler_params=_params("arbitrary", "arbitrary"),
        name="retention",
    )(rq, rkt, rv, rg, *tables)


def kernel(x, mem, even_mix_norm, even_w_in, pool_w, pool_scale, sgu_norm, sgu_w, sgu_b, even_w_out,
           odd_mix_norm, odd_w_in, odd_w_out, xattn_norm, mem_norm, xattn_wq, xattn_wkv, xattn_wo,
           ffn_norm, ffn_w_gate_up, ffn_w_down, final_norm):
    depth = xattn_wq.shape[0]
    assert x.shape[1] % (max(d for _, d in DIL_PAIRS) * DIL_BLOCK) == 0 and x.shape[2] == D_MODEL

    def row(v):
        return v.reshape(1, -1)

    kv = _memory_kv(mem, mem_norm, xattn_wkv.astype(BF16))
    h = x
    for layer in range(depth):
        i = layer // 2
        xattn_args = (kv, layer, row(xattn_norm[layer]), xattn_wq[layer].astype(BF16), xattn_wo[layer].astype(BF16))
        if layer % 2 == 0:
            w_pool = _fold_pool(pool_w[i], row(pool_scale[i]), even_w_out[i][:POOL_WIDTH])
            h = _even_mixer_xattn(h, row(even_mix_norm[i]), even_w_in[i].astype(BF16), w_pool,
                                  row(sgu_norm[i]), sgu_w[i], sgu_b[i].T, even_w_out[i].astype(BF16), *xattn_args)
        else:
            cq, ck, cv, rq, rk, rv, rg = _odd_proj(h, row(odd_mix_norm[i]), odd_w_in[i].astype(BF16))
            yc = _dilated_attention(cq, ck, cv)
            yd = _retention(rq, rk, rv, rg)
            h = _xattn(h, *xattn_args, (yc, yd, odd_w_out[i].astype(BF16)))
        h = _ffn(h, row(ffn_norm[layer]), ffn_w_gate_up[layer].astype(BF16), ffn_w_down[layer].astype(BF16),
                 row(final_norm), final_norm=layer == depth - 1)
    return h
```

```python
import functools
import math

import jax
import jax.numpy as jnp
from jax import lax
from jax.experimental import pallas as pl
from jax.experimental.pallas import tpu as pltpu

F32 = jnp.float32
BF16 = jnp.bfloat16

EPS = 1e-6
D_MODEL = 1024
N_MEM = 256

POOL_WINDOWS = (2, 4, 8, 16)
POOL_HALO = 8 * len(POOL_WINDOWS)
assert POOL_WINDOWS == tuple(2 ** (g + 1) for g in range(len(POOL_WINDOWS)))
POOL_WIDTH = 512
GROUP_DIM = 128
SGU_WIDTH = 512
SGU_CHUNK = 128
EVEN_IN = POOL_WIDTH + 2 * SGU_WIDTH

DIL_HEADS = 8
DIL_HEAD_DIM = 64
DIL_WIDTH = 512
DIL_PAIRS = ((128, 1), (512, 4), (2048, 16))
DIL_BLOCK = 128
DIL_SLOTS = 3
MASK_BIG = 2.0 ** 126
ROPE_THETA = 500000.0
ROPE_DIM = 16
RET_HEADS = 4
RET_HEAD_DIM = 128
RET_WIDTH = 512
RET_HEADS_PER_STEP = 4
RET_CHUNK = 256
RET_THETA = 10000.0
ODD_IN = 3 * DIL_WIDTH + 4 * RET_WIDTH

XATTN_HEADS = 4
XATTN_HEAD_DIM = 256
D_FF = 2816

LANES = 128
MXU_DIM = 256
FF_STEP = 6 * MXU_DIM
FF_CHUNKS = tuple((lo, min(lo + FF_STEP, D_FF)) for lo in range(0, D_FF, FF_STEP))
SEQ_TILE = 512
WIDE_TILE = 1024
EVEN_TILE = 1024
VMEM_LIMIT = 56 * 1024 * 1024


def _params(*sem):
    return pltpu.CompilerParams(dimension_semantics=sem, vmem_limit_bytes=VMEM_LIMIT)


def _const_spec(shape):
    zeros = (0,) * len(shape)
    return pl.BlockSpec(shape, lambda *_: zeros, pipeline_mode=pl.Buffered(1))


def _rms(x, g):
    ms = jnp.mean(x * x, axis=-1, keepdims=True)
    return x * lax.rsqrt(ms + EPS) * g


def _gelu_tanh(x):
    return 0.5 * x * (1.0 + jnp.tanh(math.sqrt(2.0 / math.pi) * (x + 0.044715 * (x * x * x))))


def _silu(x):
    return x * (1.0 / (1.0 + jnp.exp(-x)))


def _dot(a, b):
    return jnp.dot(a, b, preferred_element_type=F32)


def _dot_nt(a, b):
    return lax.dot_general(a, b, (((1,), (1,)), ((), ())), preferred_element_type=F32)


def _fold_pool_kernel(pw_ref, ps_ref, wo_ref, o_ref):
    for g in range(len(POOL_WINDOWS)):
        rows = slice(g * GROUP_DIM, (g + 1) * GROUP_DIM)
        o_ref[rows, :] = jnp.dot(pw_ref[g] * ps_ref[:, rows], wo_ref[rows, :], precision=lax.Precision.HIGHEST,
                                 preferred_element_type=F32).astype(BF16)


def _fold_pool(pool_w, pool_scale, w_out_pool):
    return pl.pallas_call(
        _fold_pool_kernel,
        out_shape=jax.ShapeDtypeStruct(w_out_pool.shape, BF16),
        compiler_params=pltpu.CompilerParams(vmem_limit_bytes=VMEM_LIMIT),
        name="fold_pool",
    )(pool_w, pool_scale, w_out_pool)


def _even_body(x_ref, g_ref, win_ref, wp_ref, sn_ref, sw_ref, sb_ref, wout_ref,
               o_ref, ext_ref, lva_ref, lvb_ref, u_ref, v_ref, ycat_ref):
    si = pl.program_id(1)
    ts = x_ref.shape[0]
    halo, end = POOL_HALO, POOL_HALO + ts
    half = MXU_DIM

    @pl.when(si == 0)
    def _():
        ext_ref[0:halo, :] = jnp.zeros((halo, POOL_WIDTH), F32)

    hn_rows = []
    for r in range(0, ts, ts // 4):
        hn_rows.append(_rms(x_ref[r:r + ts // 4, :], g_ref[...]).astype(BF16))
        ext_ref[halo + r:halo + r + ts // 4, :] = _dot(hn_rows[-1], win_ref[:, :POOL_WIDTH])
    hn = jnp.concatenate(hn_rows, axis=0)

    def in_proj(ref, c0, c1):
        ref[:, c0 - c1:c0 - c1 + half] = _dot(hn, win_ref[:, c0:c0 + half])

    def gelu_inplace(ref, c):
        ref[:, c:c + half] = _gelu_tanh(ref[:, c:c + half])

    bufs = (lva_ref, lvb_ref)

    def level(k):
        src = ext_ref if k == 1 else bufs[k % 2]
        dst, lo, back, c0 = bufs[(k - 1) % 2], 8 * k, 2 ** (k - 1), (k - 1) * GROUP_DIM
        dst[lo:end, c0:] = src[lo:end, c0:] + src[lo - back:end - back, c0:]

    u0, v0 = POOL_WIDTH, POOL_WIDTH + SGU_WIDTH
    in_proj(u_ref, u0, u0)
    level(1)
    level(2)
    in_proj(u_ref, u0 + half, u0)
    level(3)
    level(4)
    assert len(POOL_WINDOWS) == 4

    in_proj(v_ref, v0, v0)
    pos = si * ts + lax.broadcasted_iota(jnp.int32, (ts, 1), 0)
    for g, win in enumerate(POOL_WINDOWS):
        cols = slice(g * GROUP_DIM, (g + 1) * GROUP_DIM)
        wsum = bufs[g % 2][halo:end, cols]
        cnt = jnp.minimum(pos + 1, win).astype(F32)
        ycat_ref[:, cols] = (wsum / cnt - ext_ref[halo:end, cols]).astype(BF16)
    ext_ref[0:halo, :] = ext_ref[ts:end, :]

    in_proj(v_ref, v0 + half, v0)
    gelu_inplace(u_ref, 0)
    gelu_inplace(u_ref, half)

    acc = x_ref[...] + _dot(ycat_ref[:, :POOL_WIDTH], wp_ref[...])
    gelu_inplace(v_ref, 0)
    gelu_inplace(v_ref, half)
    vn = _rms(v_ref[...], sn_ref[...]).astype(BF16)

    nc = ts // SGU_CHUNK
    row = lax.broadcasted_iota(jnp.int32, (SGU_CHUNK, SGU_CHUNK), 0)
    col = lax.broadcasted_iota(jnp.int32, (SGU_CHUNK, SGU_CHUNK), 1)
    groups_per_tile = MXU_DIM // GROUP_DIM
    for g in range(SGU_WIDTH // GROUP_DIM):
        cols = slice(g * GROUP_DIM, (g + 1) * GROUP_DIM)
        ws = jnp.where(row >= col, sw_ref[g], 0.0).astype(BF16)
        rhs = jnp.concatenate([vn[c * SGU_CHUNK:(c + 1) * SGU_CHUNK, cols] for c in range(nc)], axis=1)
        mixed = _dot(ws, rhs) + sb_ref[:, g:g + 1]
        for c in range(nc):
            rows = slice(c * SGU_CHUNK, (c + 1) * SGU_CHUNK)
            yb = u_ref[rows, cols] * mixed[:, c * SGU_CHUNK:(c + 1) * SGU_CHUNK]
            ycat_ref[rows, POOL_WIDTH + g * GROUP_DIM:POOL_WIDTH + (g + 1) * GROUP_DIM] = yb.astype(BF16)
        if (g + 1) % groups_per_tile == 0:
            k0 = POOL_WIDTH + (g + 1 - groups_per_tile) * GROUP_DIM
            acc = acc + _dot(ycat_ref[:, k0:k0 + MXU_DIM], wout_ref[k0:k0 + MXU_DIM, :])
    o_ref[...] = acc


def _even_xattn_kernel(x_ref, g_ref, win_ref, wp_ref, sn_ref, sw_ref, sb_ref, wout_ref,
                       kv_ref, xg_ref, wq_ref, wo_ref, o_ref,
                       ext_ref, lva_ref, lvb_ref, u_ref, v_ref, ycat_ref, h_ref, q_ref, ocat_ref, s_ref, p_ref):
    _even_body(x_ref.at[0], g_ref, win_ref, wp_ref, sn_ref, sw_ref, sb_ref, wout_ref,
               h_ref, ext_ref, lva_ref, lvb_ref, u_ref, v_ref, ycat_ref)
    _xattn_body(h_ref, kv_ref, xg_ref, wq_ref, wo_ref, o_ref, q_ref, ocat_ref, s_ref, p_ref)


def _even_mixer_xattn(h, g, w_in, w_pool, sgu_norm, sgu_w, sgu_b_t, w_out, kv, layer, xg, wq, wo):
    b, s, d = h.shape
    ts = EVEN_TILE
    tile = pl.BlockSpec((1, ts, d), lambda i, j: (i, j, 0))
    return pl.pallas_call(
        _even_xattn_kernel,
        grid=(b, s // ts),
        in_specs=[
            tile,
            _const_spec((1, d)),
            _const_spec((d, EVEN_IN)),
            _const_spec((POOL_WIDTH, d)),
            _const_spec((1, SGU_WIDTH)),
            _const_spec(sgu_w.shape),
            _const_spec(sgu_b_t.shape),
            _const_spec((POOL_WIDTH + SGU_WIDTH, d)),
            pl.BlockSpec((1, 1, N_MEM, 2 * d), lambda i, j: (layer, i, 0, 0)),
            _const_spec((1, d)),
            _const_spec((d, d)),
            _const_spec((d, d)),
        ],
        out_specs=tile,
        out_shape=jax.ShapeDtypeStruct(h.shape, F32),
        scratch_shapes=[pltpu.VMEM((POOL_HALO + ts, POOL_WIDTH), F32)] * 3 + [
            pltpu.VMEM((ts, SGU_WIDTH), F32),
            pltpu.VMEM((ts, SGU_WIDTH), F32),
            pltpu.VMEM((ts, POOL_WIDTH + SGU_WIDTH), BF16),
            pltpu.VMEM((ts, d), F32),
            pltpu.VMEM((ts, d), BF16),
            pltpu.VMEM((ts, d), BF16),
            pltpu.VMEM((XATTN_HEADS, ts // 2, N_MEM), F32),
            pltpu.VMEM((XATTN_HEADS, ts // 2, N_MEM), BF16),
        ],
        compiler_params=_params("arbitrary", "arbitrary"),
        name="even_mixer_xattn",
    )(h, g, w_in, w_pool, sgu_norm, sgu_w, sgu_b_t, w_out, kv, xg, wq, wo)


def _kv_kernel(m_ref, g_ref, w_ref, o_ref):
    mn = _rms(m_ref[...], g_ref[0]).astype(BF16)
    o_ref[0] = _dot(mn, w_ref[0]).astype(BF16)


def _memory_kv(mem, mem_norm, wkv):
    b, m, d = mem.shape
    depth = wkv.shape[0]
    rows = b * m
    tm = math.gcd(rows, 2 * SEQ_TILE)
    out = pl.pallas_call(
        _kv_kernel,
        grid=(depth, rows // tm),
        in_specs=[
            pl.BlockSpec((tm, d), lambda l, i: (i, 0)),
            pl.BlockSpec((1, 1, d), lambda l, i: (l, 0, 0)),
            pl.BlockSpec((1, d, 2 * d), lambda l, i: (l, 0, 0)),
        ],
        out_specs=pl.BlockSpec((1, tm, 2 * d), lambda l, i: (l, i, 0)),
        out_shape=jax.ShapeDtypeStruct((depth, rows, 2 * d), BF16),
        compiler_params=_params("arbitrary", "arbitrary"),
        name="memory_kv",
    )(mem.reshape(rows, d), mem_norm.reshape(depth, 1, d), wkv)
    return out.reshape(depth, b, m, 2 * d)


def _xattn_body(x_ref, kv_ref, g_ref, wq_ref, wo_ref, o_ref, q_ref, ocat_ref, s_ref, p_ref):
    ts = x_ref.shape[0]
    top, bot = slice(0, ts // 2), slice(ts // 2, ts)
    q_scale = (XATTN_HEAD_DIM ** -0.5) * math.log2(math.e)

    def cols(h):
        return slice(h * XATTN_HEAD_DIM, (h + 1) * XATTN_HEAD_DIM)

    def q_proj(rows):
        hn = _rms(x_ref[rows, :], g_ref[...]).astype(BF16)
        q_ref[rows, :] = (_dot(hn, wq_ref[...]) * q_scale).astype(BF16)

    def scores(rows):
        for h in range(XATTN_HEADS):
            s_ref[h] = _dot_nt(q_ref[rows, cols(h)], kv_ref[0, 0, :, cols(h)])

    def softmax():
        for h in range(XATTN_HEADS):
            s = s_ref[h]
            p = jnp.exp2(s - jnp.max(s, axis=-1, keepdims=True))
            p_ref[h] = (p * (1.0 / jnp.sum(p, axis=-1, keepdims=True))).astype(BF16)

    def values(rows):
        for h in range(XATTN_HEADS):
            vh = kv_ref[0, 0, :, D_MODEL + h * XATTN_HEAD_DIM:D_MODEL + (h + 1) * XATTN_HEAD_DIM]
            ocat_ref[rows, cols(h)] = _dot(p_ref[h], vh).astype(BF16)

    def out_proj(rows):
        o_ref[0, rows, :] = x_ref[rows, :] + _dot(ocat_ref[rows, :], wo_ref[...])

    q_proj(top)
    scores(top)
    q_proj(bot)
    softmax()
    values(top)
    scores(bot)
    out_proj(top)
    softmax()
    values(bot)
    out_proj(bot)


def _xattn_kernel(h_ref, kv_ref, g_ref, wq_ref, wo_ref, o_ref, q_ref, ocat_ref, s_ref, p_ref):
    _xattn_body(h_ref.at[0], kv_ref, g_ref, wq_ref, wo_ref, o_ref, q_ref, ocat_ref, s_ref, p_ref)


def _mix_xattn_kernel(h_ref, yc_ref, yd_ref, wmix_ref, kv_ref, g_ref, wq_ref, wo_ref, o_ref,
                      q_ref, ocat_ref, s_ref, p_ref, x_ref):
    ts = h_ref.shape[1]
    for r in range(0, ts, ts // 2):
        rows = slice(r, r + ts // 2)
        x_ref[rows, :] = (h_ref[0, rows, :] + _dot(yc_ref[0, rows, :], wmix_ref[:DIL_WIDTH, :])
                          + _dot(yd_ref[0, rows, :], wmix_ref[DIL_WIDTH:, :]))
    _xattn_body(x_ref, kv_ref, g_ref, wq_ref, wo_ref, o_ref, q_ref, ocat_ref, s_ref, p_ref)


def _xattn(h, kv, layer, g, wq, wo, mix=None):
    b, s, d = h.shape
    ts = WIDE_TILE
    tile = pl.BlockSpec((1, ts, d), lambda i, j: (i, j, 0))
    kv_spec = pl.BlockSpec((1, 1, N_MEM, 2 * d), lambda i, j: (layer, i, 0, 0))
    tail_specs = [kv_spec, _const_spec((1, d)), _const_spec((d, d)), _const_spec((d, d))]
    if mix is None:
        kern, in_specs, args = _xattn_kernel, [tile] + tail_specs, (h, kv, g, wq, wo)
    else:
        yc, yd, w_mix = mix
        half = pl.BlockSpec((1, ts, DIL_WIDTH), lambda i, j: (i, j, 0))
        kern = _mix_xattn_kernel
        in_specs = [tile, half, half, _const_spec((d, d))] + tail_specs
        args = (h, yc, yd, w_mix, kv, g, wq, wo)
    return pl.pallas_call(
        kern,
        grid=(b, s // ts),
        in_specs=in_specs,
        out_specs=tile,
        out_shape=jax.ShapeDtypeStruct(h.shape, F32),
        scratch_shapes=[
            pltpu.VMEM((ts, d), BF16),
            pltpu.VMEM((ts, d), BF16),
            pltpu.VMEM((XATTN_HEADS, ts // 2, N_MEM), F32),
            pltpu.VMEM((XATTN_HEADS, ts // 2, N_MEM), BF16),
        ] + ([] if mix is None else [pltpu.VMEM((ts, d), F32)]),
        compiler_params=_params("arbitrary", "arbitrary"),
        name="xattn" if mix is None else "mix_xattn",
    )(*args)


def _ffn_kernel(x_ref, g_ref, wgu_ref, wd_ref, fg_ref, o_ref, *, final_norm):
    tm = x_ref.shape[0]
    hn_rows, gate_rows = [], []
    for r in range(0, tm, tm // 4):
        hn_rows.append(_rms(x_ref[r:r + tm // 4, :], g_ref[...]).astype(BF16))
        gate_rows.append(_dot(hn_rows[-1], wgu_ref[:, FF_CHUNKS[0][0]:FF_CHUNKS[0][1]]))
    hn = jnp.concatenate(hn_rows, axis=0)
    acc = None
    for c, (lo, hi) in enumerate(FF_CHUNKS):
        gate = jnp.concatenate(gate_rows, axis=0) if c == 0 else _dot(hn, wgu_ref[:, lo:hi])
        up = _dot(hn, wgu_ref[:, D_FF + lo:D_FF + hi])
        act = (_silu(gate) * up).astype(BF16)
        acc = (x_ref[...] if acc is None else acc) + _dot(act, wd_ref[lo:hi, :])
    if final_norm:
        acc = _rms(acc, fg_ref[...])
    o_ref[...] = acc


def _ffn(h, g, wgu, wd, fg, final_norm):
    b, s, d = h.shape
    t = b * s
    tm = WIDE_TILE
    tile = pl.BlockSpec((tm, d), lambda i: (i, 0))
    out = pl.pallas_call(
        functools.partial(_ffn_kernel, final_norm=final_norm),
        grid=(t // tm,),
        in_specs=[tile, _const_spec((1, d)), _const_spec((d, 2 * D_FF)), _const_spec((D_FF, d)),
                  _const_spec((1, d))],
        out_specs=tile,
        out_shape=jax.ShapeDtypeStruct((t, d), F32),
        compiler_params=_params("arbitrary"),
        name="ffn_final" if final_norm else "ffn",
    )(h.reshape(t, d), g, wgu, wd, fg)
    return out.reshape(b, s, d)


def _odd_proj_kernel(x_ref, g_ref, w_ref, wkt_ref, qc_ref, qsa_ref, qsb_ref, dc_ref, dsa_ref, dsb_ref,
                     rc_ref, rs_ref, kct_ref, kst_ref,
                     cq_ref, ck_ref, cv_ref, rq_ref, rkt_ref, rv_ref, rg_ref):
    ts = x_ref.shape[1]
    hn_rows, first_rows = [], []
    for r in range(0, ts, ts // 2):
        hn_rows.append(_rms(x_ref[0, r:r + ts // 2, :], g_ref[...]).astype(BF16))
        first_rows.append(_dot(hn_rows[-1], w_ref[:, :DIL_WIDTH]))
    hn = jnp.concatenate(hn_rows, axis=0)

    def section(i):
        if i == 0:
            return jnp.concatenate(first_rows, axis=0)
        return _dot(hn, w_ref[:, i * DIL_WIDTH:(i + 1) * DIL_WIDTH])

    def lane_blocks(z, fn):
        return jnp.concatenate([fn(z[:, j * LANES:(j + 1) * LANES]) for j in range(z.shape[1] // LANES)], axis=1)

    half = ROPE_DIM // 2

    def dil_rot(c_ref, sa_ref, sb_ref):
        c, sa, sb = c_ref[...], sa_ref[...], sb_ref[...]
        return lambda zb: zb * c + pltpu.roll(zb, half, 1) * sa + pltpu.roll(zb, LANES - half, 1) * sb

    cq_ref[0] = lane_blocks(section(0), dil_rot(qc_ref, qsa_ref, qsb_ref))
    ck_ref[0] = lane_blocks(section(1), dil_rot(dc_ref, dsa_ref, dsb_ref))
    cv_ref[0] = section(2)

    rc, rs = rc_ref[...], rs_ref[...]

    def ret_rot(zb):
        return zb * rc + pltpu.roll(zb, RET_HEAD_DIM // 2, 1) * rs

    rq_ref[0] = lane_blocks(section(3), ret_rot).astype(BF16)

    kt = _dot_nt(wkt_ref[...], hn)
    kct, kst = kct_ref[...], kst_ref[...]
    hd = RET_HEAD_DIM
    for h in range(RET_HEADS):
        zb = kt[h * hd:(h + 1) * hd, :]
        swapped = jnp.concatenate([zb[hd // 2:], zb[:hd // 2]], axis=0)
        rkt_ref[0, h * hd:(h + 1) * hd, :] = (zb * kct + swapped * kst).astype(BF16)

    rv_ref[0] = section(5).astype(BF16)
    rg_ref[0] = section(6)


def _rotary_tables(s):
    pos = jnp.arange(s, dtype=jnp.int32).astype(F32)
    lane = jnp.arange(LANES)
    half = ROPE_DIM // 2
    inv = 1.0 / jnp.power(jnp.float32(ROPE_THETA), jnp.arange(half, dtype=F32) / half)
    ang = pos[:, None] * inv[None, :]
    cos, sin = jnp.cos(ang), jnp.sin(ang)
    hl = lane % DIL_HEAD_DIM
    cos_l, sin_l = cos[:, hl % half], sin[:, hl % half]
    dc = jnp.where(hl[None] < ROPE_DIM, cos_l, 1.0)
    dsa = jnp.where((hl[None] >= half) & (hl[None] < ROPE_DIM), sin_l, 0.0)
    dsb = jnp.where(hl[None] < half, -sin_l, 0.0)
    rhalf = RET_HEAD_DIM // 2
    rinv = 1.0 / jnp.power(jnp.float32(RET_THETA), jnp.arange(rhalf, dtype=F32) / rhalf)
    rang = pos[:, None] * rinv[None, :]
    rcos, rsin = jnp.cos(rang), jnp.sin(rang)
    rc = jnp.concatenate([rcos, rcos], axis=1)
    rs = jnp.concatenate([-rsin, rsin], axis=1)
    qs = (DIL_HEAD_DIM ** -0.5) * math.log2(math.e)
    ks = RET_HEAD_DIM ** -0.5
    return (dc * qs, dsa * qs, dsb * qs, dc, dsa, dsb, rc, rs), ((rc * ks).T, (rs * ks).T)


def _odd_proj(h, g, w_in):
    b, s, d = h.shape
    ts = SEQ_TILE
    tables, tables_t = _rotary_tables(s)
    wkt = w_in[:, 3 * DIL_WIDTH + RET_WIDTH:3 * DIL_WIDTH + 2 * RET_WIDTH].T
    tile = pl.BlockSpec((1, ts, d), lambda j, i: (i, j, 0))
    tab = pl.BlockSpec((ts, LANES), lambda j, i: (j, 0))
    tab_t = pl.BlockSpec((LANES, ts), lambda j, i: (0, j))
    out = pl.BlockSpec((1, ts, DIL_WIDTH), lambda j, i: (i, j, 0))
    out_t = pl.BlockSpec((1, RET_WIDTH, ts), lambda j, i: (i, 0, j))
    f32o = jax.ShapeDtypeStruct((b, s, DIL_WIDTH), F32)
    bf16o = jax.ShapeDtypeStruct((b, s, RET_WIDTH), BF16)
    return pl.pallas_call(
        _odd_proj_kernel,
        grid=(s // ts, b),
        in_specs=[tile, _const_spec((1, d)), _const_spec((d, ODD_IN)), _const_spec((RET_WIDTH, d))]
        + [tab] * len(tables) + [tab_t] * len(tables_t),
        out_specs=[out, out, out, out, out_t, out, out],
        out_shape=[f32o, f32o, f32o, bf16o, jax.ShapeDtypeStruct((b, RET_WIDTH, s), BF16), bf16o, f32o],
        compiler_params=_params("arbitrary", "arbitrary"),
        name="odd_proj",
    )(h, g, w_in, wkt, *tables, *tables_t)


def _dilated_kernel(q_ref, k_ref, v_ref, o_ref, cls_ref, far_ref, mrg_ref, sel_ref, bias_ref, s_ref, p_ref, mcur_ref):
    s_len = q_ref.shape[1]
    blk = DIL_BLOCK
    lane = lax.broadcasted_iota(jnp.int32, (blk, LANES), 1)
    first_head = lane < DIL_HEAD_DIM

    qi = lax.broadcasted_iota(jnp.int32, (2 * blk, blk), 1)
    kr = lax.broadcasted_iota(jnp.int32, (2 * blk, blk), 0)
    sel_ref[...] = jnp.where(kr % blk == qi, 1.0, 0.0).astype(BF16)
    bias_ref[0] = jnp.where(jnp.abs(qi + blk - kr - blk // 2) <= blk // 2, 0.0, -MASK_BIG).astype(BF16)
    bias_ref[1] = jnp.where(kr <= qi, 0.0, -MASK_BIG).astype(BF16)

    (_, d_near), (_, dm), (_, df) = DIL_PAIRS
    assert d_near == 1 and df % dm == 0 and all(w // d == blk for w, d in DIL_PAIRS)
    ratio, cls_len = df // dm, s_len // dm
    for i, ref in enumerate((q_ref, k_ref, v_ref)):
        for r in range(dm):
            cls_ref[i, r * cls_len:(r + 1) * cls_len, :] = ref[0, pl.ds(r, cls_len, stride=dm), :]

    blocks = []
    for r in range(dm):
        for c in range(ratio):
            for n in range(s_len // (df * blk)):
                start = r * cls_len + c + n * ratio * blk
                blocks.append(("far", start, ratio, n > 0, pl.ds(start, blk, stride=ratio)))
    for r in range(dm):
        for n in range(cls_len // blk):
            blocks.append(("mid", r * cls_len + n * blk, 1, n > 0, pl.ds(r + n * dm * blk, blk, stride=dm)))
    for n in range(s_len // blk):
        blocks.append(("near", n * blk, 1, n > 0, pl.ds(n * blk, blk)))

    def rows(t, prev=False):
        _, start, stride, _, _ = blocks[t]
        start -= stride * blk if prev else 0
        return pl.ds(start, blk) if stride == 1 else pl.ds(start, blk, stride=stride)

    def load(i, t, prev=False):
        if blocks[t][0] == "near":
            return (q_ref, k_ref, v_ref)[i][0, rows(t, prev), :]
        return cls_ref[i, rows(t, prev), :]

    def load2(i, t):
        x = load(i, t).astype(BF16)
        return jnp.concatenate([load(i, t, True).astype(BF16), x], axis=0) if blocks[t][3] else x

    def n_keys(t):
        return 2 * blk if blocks[t][3] else blk

    def merged(m1, acc1, den1, m2, acc2, den2):
        m = jnp.maximum(m1, m2)
        w1, w2 = jnp.exp2(m1 - m), jnp.exp2(m2 - m)
        return m, w1 * acc1 + w2 * acc2, w1 * den1 + w2 * den2

    def scores(t):
        q = load(0, t)
        q2 = jnp.concatenate([jnp.where(first_head, q, 0.0), jnp.where(first_head, 0.0, q)], axis=0).astype(BF16)
        bias = bias_ref[0] if blocks[t][3] else bias_ref[1, :blk, :]
        s_ref[t % DIL_SLOTS, :, :n_keys(t)] = _dot_nt(jnp.concatenate([q2, sel_ref[...]], axis=1),
                                                      jnp.concatenate([load2(1, t), bias], axis=1))

    def softmax(t):
        s = s_ref[t % DIL_SLOTS, :, :n_keys(t)]
        m = jnp.max(s, axis=-1, keepdims=True)
        p_ref[t % DIL_SLOTS, :, :n_keys(t)] = jnp.exp2(s - m).astype(BF16)
        mcur_ref[t % DIL_SLOTS] = jnp.where(first_head, m[:blk], m[blk:])

    def values(t):
        vext = jnp.concatenate([load2(2, t), jnp.ones((n_keys(t), LANES), BF16)], axis=1)
        o2 = _dot(p_ref[t % DIL_SLOTS, :, :n_keys(t)], vext)
        acc = jnp.where(first_head, o2[:blk, :LANES], o2[blk:, :LANES])
        den = jnp.where(first_head, o2[:blk, LANES:], o2[blk:, LANES:])
        m = mcur_ref[t % DIL_SLOTS]
        kind, dst = blocks[t][0], blocks[t][4]
        if kind == "far":
            far_ref[0, dst, :], far_ref[1, dst, :], far_ref[2, dst, :] = m, acc, den
        elif kind == "mid":
            src = rows(t)
            m, acc, den = merged(m, acc, den, far_ref[0, src, :], far_ref[1, src, :], far_ref[2, src, :])
            mrg_ref[0, dst, :], mrg_ref[1, dst, :], mrg_ref[2, dst, :] = m, acc, den
        else:
            _, acc, den = merged(m, acc, den, mrg_ref[0, dst, :], mrg_ref[1, dst, :], mrg_ref[2, dst, :])
            o_ref[0, dst, :] = (acc / den).astype(BF16)

    for t in range(len(blocks) + 2):
        if t < len(blocks):
            scores(t)
        if 0 <= t - 1 < len(blocks):
            softmax(t - 1)
        if 0 <= t - 2 < len(blocks):
            values(t - 2)


def _dilated_attention(cq, ck, cv):
    b, s, _ = cq.shape
    spec = pl.BlockSpec((1, s, LANES), lambda i, p: (i, 0, p))
    nbr = len(DIL_PAIRS)
    return pl.pallas_call(
        _dilated_kernel,
        grid=(b, DIL_WIDTH // LANES),
        in_specs=[spec] * 3,
        out_specs=spec,
        out_shape=jax.ShapeDtypeStruct((b, s, DIL_WIDTH), BF16),
        scratch_shapes=[pltpu.VMEM((3, s, LANES), F32)] * 3 + [
            pltpu.VMEM((2 * DIL_BLOCK, LANES), BF16),
            pltpu.VMEM((2, 2 * DIL_BLOCK, LANES), BF16),
            pltpu.VMEM((DIL_SLOTS, 2 * DIL_BLOCK, 2 * DIL_BLOCK), F32),
            pltpu.VMEM((DIL_SLOTS, 2 * DIL_BLOCK, 2 * DIL_BLOCK), BF16),
            pltpu.VMEM((DIL_SLOTS, DIL_BLOCK, LANES), F32),
        ],
        compiler_params=_params("arbitrary", "arbitrary"),
        name="dilated_attention",
    )(cq, ck, cv)


def _retention_kernel(q_ref, kt_ref, v_ref, g_ref, dec_ref, kdec_ref, qdec_ref, o_ref,
                      y_ref, kv_ref, st_ref, s_ref, in_ref):
    s_len = q_ref.shape[1]
    hd = RET_HEAD_DIM
    c_len = RET_CHUNK
    n_chunks = s_len // c_len
    items = [(h, c) for c in range(n_chunks) for h in range(q_ref.shape[2] // hd)]

    def chunk(c):
        return slice(c * c_len, (c + 1) * c_len)

    def head(h):
        return slice(h * hd, (h + 1) * hd)

    def scores(h, c):
        ktc = kt_ref[0, head(h), chunk(c)]
        s_ref[h, c] = _dot(q_ref[0, chunk(c), head(h)], ktc)
        kv_ref[h, c] = _dot((ktc.astype(F32) * kdec_ref[h]).astype(BF16), v_ref[0, chunk(c), head(h)])

    def decay_scores(h, c):
        in_ref[h, c] = (s_ref[h, c] * dec_ref[h]).astype(BF16)

    for i in range(len(items) + 1):
        if i < len(items):
            scores(*items[i])
        if i >= 1:
            decay_scores(*items[i - 1])

    for h in range(q_ref.shape[2] // hd):
        g_chunk = qdec_ref[h, c_len - 1:c_len, :]
        state = jnp.zeros((hd, hd), F32)
        for c in range(n_chunks):
            st_ref[h, c] = state.astype(BF16)
            state = state * g_chunk + kv_ref[h, c]

    def outputs(h, c):
        y_ref[h, chunk(c), :] = (_dot(in_ref[h, c], v_ref[0, chunk(c), head(h)])
                                 + _dot(q_ref[0, chunk(c), head(h)], st_ref[h, c]) * qdec_ref[h])

    def normalise(h, c):
        y = y_ref[h, chunk(c), :]
        mu = jnp.mean(y, axis=-1, keepdims=True)
        yc = y - mu
        var = jnp.mean(yc * yc, axis=-1, keepdims=True)
        o_ref[0, chunk(c), head(h)] = (_silu(g_ref[0, chunk(c), head(h)])
                                       * (yc * lax.rsqrt(var + EPS))).astype(BF16)

    for i in range(len(items) + 1):
        if i < len(items):
            outputs(*items[i])
        if i >= 1:
            normalise(*items[i - 1])


def _retention_tables():
    c_len = RET_CHUNK
    gamma = 1.0 - jnp.power(2.0, -5.0 - jnp.arange(RET_HEADS, dtype=F32))
    log_g = jnp.log(gamma)
    idx = jnp.arange(c_len, dtype=F32)
    rel = idx[:, None] - idx[None, :]
    decay = jnp.where(rel[None] >= 0, jnp.exp(jnp.maximum(rel, 0.0)[None] * log_g[:, None, None]), 0.0)
    k_decay = jnp.exp((c_len - 1 - idx)[None, :] * log_g[:, None])
    q_decay = jnp.exp((idx + 1)[None, :] * log_g[:, None])
    return (decay, jnp.broadcast_to(k_decay[:, None, :], (RET_HEADS, RET_HEAD_DIM, c_len)),
            jnp.broadcast_to(q_decay[:, :, None], (RET_HEADS, c_len, RET_HEAD_DIM)))


def _retention(rq, rkt, rv, rg):
    b, s, _ = rq.shape
    hd, hps, nc = RET_HEAD_DIM, RET_HEADS_PER_STEP, s // RET_CHUNK
    spec = pl.BlockSpec((1, s, hps * hd), lambda i, h: (i, 0, h))
    spec_t = pl.BlockSpec((1, hps * hd, s), lambda i, h: (i, h, 0))
    tables = _retention_tables()
    return pl.pallas_call(
        _retention_kernel,
        grid=(b, RET_HEADS // hps),
        in_specs=[spec, spec_t, spec, spec]
        + [pl.BlockSpec((hps,) + t.shape[1:], lambda i, h: (h, 0, 0)) for t in tables],
        out_specs=spec,
        out_shape=jax.ShapeDtypeStruct((b, s, RET_WIDTH), BF16),
        scratch_shapes=[
            pltpu.VMEM((hps, s, hd), F32),
            pltpu.VMEM((hps, nc, hd, hd), F32),
            pltpu.VMEM((hps, nc, hd, hd), BF16),
            pltpu.VMEM((hps, nc, RET_CHUNK, RET_CHUNK), F32),
            pltpu.VMEM((hps, nc, RET_CHUNK, RET_CHUNK), BF16),
        ],
        compiler_params=_params("arbitrary", "arbitrary"),
        name="retention",
    )(rq, rkt, rv, rg, *tables)


def kernel(x, mem, even_mix_norm, even_w_in, pool_w, pool_scale, sgu_norm, sgu_w, sgu_b, even_w_out,
           odd_mix_norm, odd_w_in, odd_w_out, xattn_norm, mem_norm, xattn_wq, xattn_wkv, xattn_wo,
           ffn_norm, ffn_w_gate_up, ffn_w_down, final_norm):
    depth = xattn_wq.shape[0]
    assert x.shape[1] % (max(d for _, d in DIL_PAIRS) * DIL_BLOCK) == 0 and x.shape[2] == D_MODEL

    def row(v):
        return v.reshape(1, -1)

    kv = _memory_kv(mem, mem_norm, xattn_wkv.astype(BF16))
    h = x
    for layer in range(depth):
        i = layer // 2
        xattn_args = (kv, layer, row(xattn_norm[layer]), xattn_wq[layer].astype(BF16), xattn_wo[layer].astype(BF16))
        if layer % 2 == 0:
            w_pool = _fold_pool(pool_w[i], row(pool_scale[i]), even_w_out[i][:POOL_WIDTH])
            h = _even_mixer_xattn(h, row(even_mix_norm[i]), even_w_in[i].astype(BF16), w_pool,
                                  row(sgu_norm[i]), sgu_w[i], sgu_b[i].T, even_w_out[i].astype(BF16), *xattn_args)
        else:
            cq, ck, cv, rq, rk, rv, rg = _odd_proj(h, row(odd_mix_norm[i]), odd_w_in[i].astype(BF16))
            yc = _dilated_attention(cq, ck, cv)
            yd = _retention(rq, rk, rv, rg)
            h = _xattn(h, *xattn_args, (yc, yd, odd_w_out[i].astype(BF16)))
        h = _ffn(h, row(ffn_norm[layer]), ffn_w_gate_up[layer].astype(BF16), ffn_w_down[layer].astype(BF16),
                 row(final_norm), final_norm=layer == depth - 1)
    return h
```
